```python
import jax, jax.numpy as jnp
from jax import lax
import numpy as np

D_MODEL = 2048
BATCH = 1
SEQ = 8192
DEPTH = 4

GRID_W = 64
CTX_LEN = 256
N_MIXERS = 2
MIXER_MLA = 0
MIXER_FOURIER = 1
N_MLA_LAYERS = (DEPTH + 1) // 2
N_FNO_LAYERS = DEPTH // 2
N_HEADS = 16
Q_LORA = 512
KV_LORA = 512
QK_NOPE = 128
QK_ROPE = 64
V_DIM = 128
ROPE_BASE = 10000.0
Q_BLOCK = 128
SM_SCALE = (QK_NOPE + QK_ROPE) ** -0.5
FNO_GROUPS = 8
FNO_GROUP_DIM = D_MODEL // FNO_GROUPS
D_FF = 5632
CONV_W = 3
NORM_EPS = 1e-6
N_MOD = 6

kernel_name = "hybrid_mla_fnet_convffn_dit"


def _rmsnorm(x, g):
    xf = x.astype(jnp.float32)
    y = xf * lax.rsqrt(jnp.mean(xf * xf, axis=-1, keepdims=True) + NORM_EPS)
    return (y * g.astype(jnp.float32)).astype(x.dtype)


def _ada(cond, w, b):
    m = jax.nn.silu(cond) @ w + b
    m = m.reshape(m.shape[:-1] + (N_MOD, D_MODEL))
    return jnp.split(m, N_MOD, axis=-2)


def _modulate(h, shift, scale):
    return h * (1 + scale) + shift


def _axial_rope_tables(n, dtype):
    rows = n // GRID_W
    r, col = jnp.meshgrid(jnp.arange(rows, dtype=jnp.float32),
                          jnp.arange(GRID_W, dtype=jnp.float32), indexing="ij")
    r = r.reshape(-1)
    col = col.reshape(-1)
    half = QK_ROPE // 2
    inv_freq = jnp.power(ROPE_BASE, -jnp.arange(0, half, 2, dtype=jnp.float32) / half)
    ang_r = r[:, None] * inv_freq
    ang_c = col[:, None] * inv_freq
    ang = jnp.concatenate([ang_r, ang_r, ang_c, ang_c], axis=-1)
    return jnp.cos(ang).astype(dtype), jnp.sin(ang).astype(dtype)


def _rotate_axial(u):
    half = QK_ROPE // 2
    q = half // 2
    a1, a2 = u[..., :q], u[..., q:half]
    b1, b2 = u[..., half:half + q], u[..., half + q:]
    return jnp.concatenate([-a2, a1, -b2, b1], axis=-1)


def _apply_rope(u, cos, sin):
    return u * cos + _rotate_axial(u) * sin


def _mla_down(h, w_dqkv):
    d = h @ w_dqkv
    return d[..., :Q_LORA], d[..., Q_LORA:Q_LORA + KV_LORA], d[..., Q_LORA + KV_LORA:]


def _mla_queries(c_q, g_q, w_uq):
    b, n, _ = c_q.shape
    q = (_rmsnorm(c_q, g_q) @ w_uq).reshape(b, n, N_HEADS, QK_NOPE + QK_ROPE)
    return q[..., :QK_NOPE], q[..., QK_NOPE:]


def _mla_keys(c_kv, g_kv, w_ukv):
    b, n, _ = c_kv.shape
    kv = (_rmsnorm(c_kv, g_kv) @ w_ukv).reshape(b, n, N_HEADS, QK_NOPE + V_DIM)
    return kv[..., :QK_NOPE], kv[..., QK_NOPE:]


def _attend(qn, qr, kn, kr, v):
    s = jnp.einsum("bqhd,bkhd->bhqk", qn, kn) + jnp.einsum("bqhr,bkr->bhqk", qr, kr)
    p = jax.nn.softmax(s.astype(jnp.float32) * SM_SCALE, axis=-1).astype(v.dtype)
    return jnp.einsum("bhqk,bkhd->bqhd", p, v)


def _mla(hx, hc, w_dqkv, g_q, g_kv, w_uq, w_ukv, w_o, cos, sin, ctx_out):
    b, n, _ = hx.shape
    cq_x, ckv_x, kr_x = _mla_down(hx, w_dqkv)
    qn_x, qr_x = _mla_queries(cq_x, g_q, w_uq)
    qr_x = _apply_rope(qr_x, cos[None, :, None, :], sin[None, :, None, :])
    kn_x, v_x = _mla_keys(ckv_x, g_kv, w_ukv)
    kr_x = _apply_rope(kr_x, cos[None], sin[None])
    cq_c, ckv_c, kr_c = _mla_down(hc, w_dqkv)
    kn_c, v_c = _mla_keys(ckv_c, g_kv, w_ukv)
    kn = jnp.concatenate([kn_c, kn_x], axis=1)
    kr = jnp.concatenate([kr_c, kr_x], axis=1)
    v = jnp.concatenate([v_c, v_x], axis=1)
    nb = n // Q_BLOCK

    def to_blocks(t):
        return jnp.moveaxis(t.reshape((b, nb, Q_BLOCK) + t.shape[2:]), 1, 0)

    def block(qs):
        qn_b, qr_b = qs
        return _attend(qn_b, qr_b, kn, kr, v)

    o = lax.map(block, (to_blocks(qn_x), to_blocks(qr_x)))
    o = jnp.moveaxis(o, 0, 1).reshape(b, n, N_HEADS * V_DIM)
    yx = o @ w_o
    yc = None
    if ctx_out:
        qn_c, qr_c = _mla_queries(cq_c, g_q, w_uq)
        oc = _attend(qn_c, qr_c, kn_c, kr_c, v_c)
        yc = oc.reshape(b, hc.shape[1], N_HEADS * V_DIM) @ w_o
    return yx, yc


def _fourier_mix(h, w):
    b, n, _ = h.shape
    hg = h.astype(jnp.float32).reshape(b, n, FNO_GROUPS, FNO_GROUP_DIM)
    f = jnp.fft.fft2(hg, axes=(1, 3), norm="ortho").real
    return f.reshape(b, n, D_MODEL).astype(h.dtype) @ w


def _dwconv3(u, w, bias):
    up = jnp.pad(u, ((0, 0), (1, 1), (0, 0)))
    return up[:, :-2] * w[0] + up[:, 1:-1] * w[1] + up[:, 2:] * w[2] + bias


def _conv_ffn(h, w_up, conv_w, conv_b, w_down):
    u = _dwconv3(h @ w_up, conv_w, conv_b)
    val, gate = jnp.split(u, 2, axis=-1)
    return (val * jax.nn.silu(gate)) @ w_down


def setup_inputs(seed: int = 0) -> dict:
    key = jax.random.key(seed)
    ks = jax.random.split(key, 20)
    nrm = jax.random.normal
    f32 = jnp.float32
    D = D_MODEL
    return {
        "x": nrm(ks[0], (BATCH, SEQ, D), f32),
        "c": nrm(ks[1], (BATCH, D), f32),
        "ctx": nrm(ks[2], (BATCH, CTX_LEN, D), f32),
        "c_ctx": nrm(ks[3], (D,), f32),
        "ada_w": nrm(ks[4], (DEPTH, D, N_MOD * D), f32) * D ** -0.5,
        "ada_b": nrm(ks[5], (DEPTH, N_MOD * D), f32) * 0.02,
        "norm1_g": 1.0 + 0.02 * nrm(ks[6], (DEPTH, D), f32),
        "norm2_g": 1.0 + 0.02 * nrm(ks[7], (DEPTH, D), f32),
        "mla_w_dqkv": nrm(ks[8], (N_MLA_LAYERS, D, Q_LORA + KV_LORA + QK_ROPE), f32) * D ** -0.5,
        "mla_q_norm_g": 1.0 + 0.02 * nrm(ks[9], (N_MLA_LAYERS, Q_LORA), f32),
        "mla_kv_norm_g": 1.0 + 0.02 * nrm(ks[10], (N_MLA_LAYERS, KV_LORA), f32),
        "mla_w_uq": nrm(ks[11], (N_MLA_LAYERS, Q_LORA, N_HEADS * (QK_NOPE + QK_ROPE)), f32) * Q_LORA ** -0.5,
        "mla_w_ukv": nrm(ks[12], (N_MLA_LAYERS, KV_LORA, N_HEADS * (QK_NOPE + V_DIM)), f32) * KV_LORA ** -0.5,
        "mla_w_o": nrm(ks[13], (N_MLA_LAYERS, N_HEADS * V_DIM, D), f32) * (N_HEADS * V_DIM) ** -0.5,
        "fno_w": nrm(ks[14], (N_FNO_LAYERS, D, D), f32) * D ** -0.5,
        "ffn_w_up": nrm(ks[15], (DEPTH, D, 2 * D_FF), f32) * D ** -0.5,
        "ffn_conv_w": nrm(ks[16], (DEPTH, CONV_W, 2 * D_FF), f32) * CONV_W ** -0.5,
        "ffn_conv_b": nrm(ks[17], (DEPTH, 2 * D_FF), f32) * 0.02,
        "ffn_w_down": nrm(ks[18], (DEPTH, D_FF, D), f32) * D_FF ** -0.5,
        "final_norm_g": 1.0 + 0.02 * nrm(ks[19], (D,), f32),
    }


def reference(x, c, ctx, c_ctx, ada_w, ada_b, norm1_g, norm2_g, mla_w_dqkv, mla_q_norm_g,
              mla_kv_norm_g, mla_w_uq, mla_w_ukv, mla_w_o, fno_w, ffn_w_up, ffn_conv_w,
              ffn_conv_b, ffn_w_down, final_norm_g):
    n = x.shape[1]
    cos, sin = _axial_rope_tables(n, x.dtype)
    for i in range(DEPTH):
        kind = i % N_MIXERS
        j = i // N_MIXERS
        ctx_later = any(l % N_MIXERS == MIXER_MLA for l in range(i + 1, DEPTH))
        sh1, sc1, g1, sh2, sc2, g2 = _ada(c, ada_w[i], ada_b[i])
        hx = _modulate(_rmsnorm(x, norm1_g[i]), sh1, sc1)
        hc = None
        if kind == MIXER_MLA or ctx_later:
            csh1, csc1, cg1, csh2, csc2, cg2 = _ada(c_ctx, ada_w[i], ada_b[i])
            hc = _modulate(_rmsnorm(ctx, norm1_g[i]), csh1, csc1)
        if kind == MIXER_MLA:
            yx, yc = _mla(hx, hc, mla_w_dqkv[j], mla_q_norm_g[j], mla_kv_norm_g[j], mla_w_uq[j],
                          mla_w_ukv[j], mla_w_o[j], cos, sin, ctx_later)
        else:
            yx = _fourier_mix(hx, fno_w[j])
            yc = _fourier_mix(hc, fno_w[j]) if ctx_later else None
        x = x + g1 * yx
        x = x + g2 * _conv_ffn(_modulate(_rmsnorm(x, norm2_g[i]), sh2, sc2),
                               ffn_w_up[i], ffn_conv_w[i], ffn_conv_b[i], ffn_w_down[i])
        if ctx_later:
            ctx = ctx + cg1 * yc
            ctx = ctx + cg2 * _conv_ffn(_modulate(_rmsnorm(ctx, norm2_g[i]), csh2, csc2),
                                        ffn_w_up[i], ffn_conv_w[i], ffn_conv_b[i], ffn_w_down[i])
    return _rmsnorm(x, final_norm_g)
```

```python
import functools
import math

import jax
import jax.numpy as jnp
import numpy as np
from jax import lax
from jax.experimental import pallas as pl
from jax.experimental.pallas import tpu as pltpu

D_MODEL = 2048
DEPTH = 4
GRID_W = 64
N_MIXERS = 2
MIXER_MLA = 0
N_HEADS = 16
Q_LORA = 512
KV_LORA = 512
QK_NOPE = 128
QK_ROPE = 64
V_DIM = 128
ROPE_BASE = 10000.0
SM_SCALE = (QK_NOPE + QK_ROPE) ** -0.5
FNO_GROUPS = 8
FNO_GROUP_DIM = D_MODEL // FNO_GROUPS
D_FF = 5632
NORM_EPS = 1e-6
N_MOD = 6

LANES = 128
SUBLANES = 8
BF16_ROWS = 16
VMEM_LIMIT_BYTES = 56 * 1024 * 1024

HEAD_PAD = 2 * LANES
FFT_N2 = 128

BF16 = jnp.bfloat16
F32 = jnp.float32
HIGHEST = lax.Precision.HIGHEST

SH1, SC1, G1, SH2, SC2, G2 = range(6)


def _params(*sem):
    return pltpu.CompilerParams(dimension_semantics=sem, vmem_limit_bytes=VMEM_LIMIT_BYTES)


def _const_spec(shape):
    nd = len(shape)
    return pl.BlockSpec(shape, lambda *_: (0,) * nd, pipeline_mode=pl.Buffered(1))


def _dot(a, b):
    return jnp.dot(a, b, preferred_element_type=F32)


def _dot_hi(a, b):
    return jnp.dot(a, b, preferred_element_type=F32, precision=HIGHEST)


def _rms(x, g):
    return x * lax.rsqrt(jnp.mean(x * x, axis=-1, keepdims=True) + NORM_EPS) * g


def _norm_mod(x, g, shift, scale):
    return _rms(x, g) * (1.0 + scale) + shift


def _ada_kernel(condT_ref, w_ref, b_ref, o_ref, acc_ref):
    k = pl.program_id(2)
    tk, tn = w_ref.shape[1], w_ref.shape[2]
    n_rows = acc_ref.shape[0]

    @pl.when(k == 0)
    def _():
        acc_ref[...] = jnp.zeros_like(acc_ref)

    w = w_ref[0]
    ct = condT_ref[pl.ds(pl.multiple_of(k * tk, tk), tk), :]
    s = ct * (1.0 / (1.0 + jnp.exp(-ct)))
    for r in range(n_rows):
        prod = w * s[:, r:r + 1]
        acc_ref[r] += prod.reshape(tk // SUBLANES, SUBLANES, tn).sum(axis=0)

    @pl.when(k == pl.num_programs(2) - 1)
    def _():
        for r in range(n_rows):
            o_ref[0, r:r + 1, :] = acc_ref[r].sum(axis=0, keepdims=True) + b_ref[0]


def _ada_all(cond, ada_w, ada_b):
    n_rows, d = cond.shape
    depth, _, nout = ada_w.shape
    tk, tn = 512, 2048
    out = pl.pallas_call(
        _ada_kernel,
        grid=(depth, nout // tn, d // tk),
        in_specs=[
            pl.BlockSpec((d, n_rows), lambda l, j, k: (0, 0)),
            pl.BlockSpec((1, tk, tn), lambda l, j, k: (l, k, j)),
            pl.BlockSpec((1, 1, tn), lambda l, j, k: (l, 0, j)),
        ],
        out_specs=pl.BlockSpec((1, n_rows, tn), lambda l, j, k: (l, 0, j)),
        out_shape=jax.ShapeDtypeStruct((depth, n_rows, nout), F32),
        scratch_shapes=[pltpu.VMEM((n_rows, SUBLANES, tn), F32)],
        compiler_params=_params("arbitrary", "arbitrary", "arbitrary"),
        name="ada_mod",
    )(cond.T, ada_w, ada_b.reshape(depth, 1, nout))
    return out.reshape(depth, n_rows, N_MOD, d)


def _mla_proj_kernel(x_ref, mod_ref, g1_ref, wd_ref, gq_ref, gkv_ref, wq_ref, wk_ref, wvT_ref,
                     cos_ref, sin_ref, q_ref, k_ref, vT_ref):
    hd = N_HEADS * LANES
    h = _norm_mod(x_ref[...], g1_ref[...], mod_ref[SH1:SH1 + 1], mod_ref[SC1:SC1 + 1])
    d = _dot(h.astype(BF16), wd_ref[...])
    cq = _rms(d[:, :Q_LORA], gq_ref[...]).astype(BF16)
    ckv = _rms(d[:, Q_LORA:Q_LORA + KV_LORA], gkv_ref[...]).astype(BF16)
    cos = cos_ref[...]
    sin = sin_ref[...]
    base = Q_LORA + KV_LORA
    kr = (d[:, base:base + LANES] * cos + d[:, base + LANES:base + 2 * LANES] * sin).astype(BF16)
    qall = _dot(cq, wq_ref[...])
    kn = _dot(ckv, wk_ref[...])
    vT = lax.dot_general(wvT_ref[...], ckv, (((1,), (1,)), ((), ())),
                         preferred_element_type=F32)
    for hh in range(N_HEADS):
        lo, hi = hh * LANES, (hh + 1) * LANES
        qr = qall[:, hd + lo:hd + hi] * cos + qall[:, 2 * hd + lo:2 * hd + hi] * sin
        q_ref[hh, :, 0:LANES] = (qall[:, lo:hi] * SM_SCALE).astype(BF16)
        q_ref[hh, :, LANES:HEAD_PAD] = (qr * SM_SCALE).astype(BF16)
        k_ref[hh, :, 0:LANES] = kn[:, lo:hi].astype(BF16)
        k_ref[hh, :, LANES:HEAD_PAD] = kr
        vT_ref[hh] = vT[lo:hi, :].astype(BF16)


def _mla_proj(x, mod, g1, wts, cos, sin, tm):
    n, d = x.shape
    wd, gq, gkv, wq, wk, wvT = wts
    return pl.pallas_call(
        _mla_proj_kernel,
        grid=(n // tm,),
        in_specs=[
            pl.BlockSpec((tm, d), lambda i: (i, 0)),
            _const_spec(mod.shape), _const_spec(g1.shape), _const_spec(wd.shape),
            _const_spec(gq.shape), _const_spec(gkv.shape), _const_spec(wq.shape),
            _const_spec(wk.shape), _const_spec(wvT.shape),
            pl.BlockSpec((tm, LANES), lambda i: (i, 0)),
            pl.BlockSpec((tm, LANES), lambda i: (i, 0)),
        ],
        out_specs=[
            pl.BlockSpec((N_HEADS, tm, HEAD_PAD), lambda i: (0, i, 0)),
            pl.BlockSpec((N_HEADS, tm, HEAD_PAD), lambda i: (0, i, 0)),
            pl.BlockSpec((N_HEADS, V_DIM, tm), lambda i: (0, 0, i)),
        ],
        out_shape=[
            jax.ShapeDtypeStruct((N_HEADS, n, HEAD_PAD), BF16),
            jax.ShapeDtypeStruct((N_HEADS, n, HEAD_PAD), BF16),
            jax.ShapeDtypeStruct((N_HEADS, V_DIM, n), BF16),
        ],
        compiler_params=_params("arbitrary"),
        name="mla_proj",
    )(x, mod, g1, wd, gq, gkv, wq, wk, wvT, cos, sin)


def _flash_kernel(*refs, seg_lens, tk):
    nseg = len(seg_lens)
    q_ref = refs[0]
    kv_refs = refs[1:1 + 2 * nseg]
    o_ref = refs[1 + 2 * nseg]
    m_ref, l_ref, acc_ref = refs[2 + 2 * nseg:]
    q = q_ref[0]
    m_ref[...] = jnp.full_like(m_ref, -jnp.inf)
    l_ref[...] = jnp.zeros_like(l_ref)
    acc_ref[...] = jnp.zeros_like(acc_ref)

    for si in range(nseg):
        k_ref, vT_ref = kv_refs[2 * si], kv_refs[2 * si + 1]
        ck = min(tk, seg_lens[si])

        def body(c, carry, k_ref=k_ref, vT_ref=vT_ref, ck=ck):
            off = pl.multiple_of(c * ck, ck)
            k = k_ref[0, pl.ds(off, ck), :]
            s = lax.dot_general(k, q, (((1,), (1,)), ((), ())), preferred_element_type=F32)
            m_old = m_ref[...]
            m_new = jnp.maximum(m_old, jnp.max(s, axis=0, keepdims=True))
            alpha = jnp.exp(m_old - m_new)
            p = jnp.exp(s - m_new)
            l_ref[...] = alpha * l_ref[...] + jnp.sum(p, axis=0, keepdims=True)
            vT = vT_ref[0, :, pl.ds(off, ck)]
            acc_ref[...] = alpha * acc_ref[...] + _dot(vT, p.astype(BF16))
            m_ref[...] = m_new
            return carry

        lax.fori_loop(0, seg_lens[si] // ck, body, 0)

    o = acc_ref[...] * (1.0 / l_ref[...])
    o_ref[...] = o.T.astype(o_ref.dtype)


def _flash(q, kvs, tq, tk):
    nh, n, _ = q.shape
    seg_lens = tuple(k.shape[1] for k, _ in kvs)
    in_specs = [pl.BlockSpec((1, tq, HEAD_PAD), lambda h, i: (h, i, 0))]
    args = [q]
    for k, vT in kvs:
        nk = k.shape[1]
        in_specs.append(pl.BlockSpec((1, nk, HEAD_PAD), lambda h, i: (h, 0, 0)))
        in_specs.append(pl.BlockSpec((1, V_DIM, nk), lambda h, i: (h, 0, 0)))
        args += [k, vT]
    return pl.pallas_call(
        functools.partial(_flash_kernel, seg_lens=seg_lens, tk=tk),
        grid=(nh, n // tq),
        in_specs=in_specs,
        out_specs=pl.BlockSpec((tq, V_DIM), lambda h, i: (i, h)),
        out_shape=jax.ShapeDtypeStruct((n, nh * V_DIM), BF16),
        scratch_shapes=[pltpu.VMEM((1, tq), F32), pltpu.VMEM((1, tq), F32),
                        pltpu.VMEM((V_DIM, tq), F32)],
        compiler_params=_params("arbitrary", "arbitrary"),
        name="mla_flash",
    )(*args)


def _outproj_kernel(o_ref, w_ref, x_ref, mod_ref, out_ref):
    out_ref[...] = x_ref[...] + mod_ref[G1:G1 + 1] * _dot(o_ref[...], w_ref[...])


def _outproj(o, w, x, mod, tm):
    n, d = x.shape
    return pl.pallas_call(
        _outproj_kernel,
        grid=(n // tm,),
        in_specs=[
            pl.BlockSpec((tm, o.shape[1]), lambda i: (i, 0)),
            _const_spec(w.shape),
            pl.BlockSpec((tm, d), lambda i: (i, 0)),
            _const_spec(mod.shape),
        ],
        out_specs=pl.BlockSpec((tm, d), lambda i: (i, 0)),
        out_shape=jax.ShapeDtypeStruct((n, d), F32),
        compiler_params=_params("arbitrary"),
        name="mla_outproj",
    )(o, w, x, mod)


def _ffn_kernel(xp_ref, x_ref, xn_ref, mod_ref, g2_ref, wuv_ref, wug_ref, cwv_ref, cwg_ref,
                cbv_ref, cbg_ref, wd_ref, fg_ref, o_ref, h_ref, acc_ref, *, tm, final):
    i = pl.program_id(0)
    f = pl.program_id(1)
    halo = BF16_ROWS
    rows = tm + 2 * halo

    @pl.when(f == 0)
    def _():
        g, sh, sc = g2_ref[...], mod_ref[SH2:SH2 + 1], mod_ref[SC2:SC2 + 1]
        hp = jnp.where(i > 0, _norm_mod(xp_ref[...], g, sh, sc), 0.0)
        hn = jnp.where(i < pl.num_programs(0) - 1, _norm_mod(xn_ref[...], g, sh, sc), 0.0)
        h_ref[0:halo] = hp.astype(BF16)
        h_ref[halo:halo + tm] = _norm_mod(x_ref[...], g, sh, sc).astype(BF16)
        h_ref[halo + tm:rows] = hn.astype(BF16)
        acc_ref[...] = jnp.zeros_like(acc_ref)

    hh = h_ref[...]

    def conv(w_ref, cw_ref, cb_ref):
        u = _dot(hh, w_ref[...])
        cw = cw_ref[...]
        r = (pltpu.roll(u, 1, 0) * cw[0:1] + u * cw[1:2] + pltpu.roll(u, rows - 1, 0) * cw[2:3]
             + cb_ref[...])
        return r[halo:halo + tm]

    val = conv(wuv_ref, cwv_ref, cbv_ref)
    gate = conv(wug_ref, cwg_ref, cbg_ref)
    act = val * (gate * (1.0 / (1.0 + jnp.exp(-gate))))
    acc_ref[...] += _dot(act.astype(BF16), wd_ref[...])

    @pl.when(f == pl.num_programs(1) - 1)
    def _():
        out = x_ref[...] + mod_ref[G2:G2 + 1] * acc_ref[...]
        if final:
            out = _rms(out, fg_ref[...])
        o_ref[...] = out


def _ffn(x, mod, g2, w_up, conv_w, conv_b, w_down, final_g, tm, tf, final):
    n, d = x.shape
    dff = w_down.shape[0]
    nf = dff // tf
    halo = BF16_ROWS
    hb = tm // halo
    last = n // halo - 1
    return pl.pallas_call(
        functools.partial(_ffn_kernel, tm=tm, final=final),
        grid=(n // tm, nf),
        in_specs=[
            pl.BlockSpec((halo, d), lambda i, f: (jnp.maximum(i * hb - 1, 0), 0)),
            pl.BlockSpec((tm, d), lambda i, f: (i, 0)),
            pl.BlockSpec((halo, d), lambda i, f: (jnp.minimum((i + 1) * hb, last), 0)),
            _const_spec(mod.shape), _const_spec(g2.shape),
            pl.BlockSpec((d, tf), lambda i, f: (0, f)),
            pl.BlockSpec((d, tf), lambda i, f: (0, nf + f)),
            pl.BlockSpec((3, tf), lambda i, f: (0, f)),
            pl.BlockSpec((3, tf), lambda i, f: (0, nf + f)),
            pl.BlockSpec((1, tf), lambda i, f: (0, f)),
            pl.BlockSpec((1, tf), lambda i, f: (0, nf + f)),
            pl.BlockSpec((tf, d), lambda i, f: (f, 0)),
            _const_spec(final_g.shape),
        ],
        out_specs=pl.BlockSpec((tm, d), lambda i, f: (i, 0)),
        out_shape=jax.ShapeDtypeStruct((n, d), F32),
        scratch_shapes=[pltpu.VMEM((tm + 2 * halo, d), BF16), pltpu.VMEM((tm, d), F32)],
        compiler_params=_params("arbitrary", "arbitrary"),
        name="conv_ffn",
    )(x, x, x, mod, g2, w_up, w_up, conv_w, conv_w, conv_b, conv_b, w_down, final_g)


def _fourier_tail(xr, xi, cc_ref, sc_ref, w_ref, x_ref, mod_ref, o_ref):
    parts = []
    for g in range(FNO_GROUPS):
        lo, hi = g * FNO_GROUP_DIM, (g + 1) * FNO_GROUP_DIM
        parts.append(_dot_hi(xr[:, lo:hi], cc_ref[...]) + _dot_hi(xi[:, lo:hi], sc_ref[...]))
    fmix = jnp.concatenate(parts, axis=1).astype(BF16)
    o_ref[...] = x_ref[...] + mod_ref[G1:G1 + 1] * _dot(fmix, w_ref[...])


def _fft_stage1_kernel(x_ref, mod_ref, g1_ref, c_ref, s_ref, yr_ref, yi_ref, *, b2, d):
    for q in range(b2):
        sl = slice(q * d, (q + 1) * d)
        h = _norm_mod(x_ref[:, sl], g1_ref[...], mod_ref[SH1:SH1 + 1], mod_ref[SC1:SC1 + 1])
        yr_ref[:, sl] = _dot_hi(c_ref[...], h)
        yi_ref[:, sl] = _dot_hi(s_ref[...], h)


def _fft_stage2_kernel(yr_ref, yi_ref, m_ref, cc_ref, sc_ref, w_ref, x_ref, mod_ref, o_ref):
    ys = jnp.concatenate([yr_ref[...], yi_ref[...]], axis=0)
    xs = _dot_hi(m_ref[0], ys)
    _fourier_tail(xs[:FFT_N2], xs[FFT_N2:], cc_ref, sc_ref, w_ref, x_ref, mod_ref, o_ref)


def _dft_small_kernel(x_ref, mod_ref, g1_ref, m_ref, cc_ref, sc_ref, w_ref, o_ref):
    n = x_ref.shape[0]
    x = x_ref[...]
    h = _norm_mod(x, g1_ref[...], mod_ref[SH1:SH1 + 1], mod_ref[SC1:SC1 + 1])
    xs = _dot_hi(m_ref[...], h)
    _fourier_tail(xs[:n], xs[n:], cc_ref, sc_ref, w_ref, x_ref, mod_ref, o_ref)


def _dft_tables(n):
    def cs(num, den, scale):
        ang = (2.0 * np.pi / den) * (num % den).astype(np.float64)
        return (np.cos(ang) * scale).astype(np.float32), (np.sin(ang) * scale).astype(np.float32)

    gd = FNO_GROUP_DIM
    a = np.arange(gd, dtype=np.int64)
    cc, sc = cs(a[:, None] * a[None, :], gd, gd ** -0.5)
    tabs = {"cc": cc, "sc": sc}
    if n % FFT_N2 == 0 and n > 2 * FFT_N2:
        n1 = n // FFT_N2
        j1 = np.arange(n1, dtype=np.int64)
        c1, s1 = cs(j1[:, None] * j1[None, :], n1, n1 ** -0.5)
        tabs["c1"], tabs["ms1"] = c1, -s1
        k = (j1[:, None, None] + n1 * np.arange(FFT_N2, dtype=np.int64)[None, :, None])
        j2 = np.arange(FFT_N2, dtype=np.int64)[None, None, :]
        mc, ms = cs(k * j2, n, FFT_N2 ** -0.5)
        tabs["m2"] = np.concatenate([np.concatenate([mc, ms], axis=2),
                                     np.concatenate([-ms, mc], axis=2)], axis=1)
    else:
        j = np.arange(n, dtype=np.int64)
        c, s = cs(j[:, None] * j[None, :], n, n ** -0.5)
        tabs["m"] = np.concatenate([c, -s], axis=0)
    return {k_: jnp.asarray(v) for k_, v in tabs.items()}


def _fourier_long(x, mod, g1, w, tabs, b2):
    n, d = x.shape
    n1 = n // FFT_N2
    x2 = x.reshape(n1, FFT_N2 * d)
    yr, yi = pl.pallas_call(
        functools.partial(_fft_stage1_kernel, b2=b2, d=d),
        grid=(FFT_N2 // b2,),
        in_specs=[
            pl.BlockSpec((n1, b2 * d), lambda s: (0, s)),
            _const_spec(mod.shape), _const_spec(g1.shape),
            _const_spec((n1, n1)), _const_spec((n1, n1)),
        ],
        out_specs=[pl.BlockSpec((n1, b2 * d), lambda s: (0, s))] * 2,
        out_shape=[jax.ShapeDtypeStruct((n1, FFT_N2 * d), F32)] * 2,
        compiler_params=_params("arbitrary"),
        name="fft_stage1",
    )(x2, mod, g1, tabs["c1"], tabs["ms1"])
    yr = yr.reshape(n, d)
    yi = yi.reshape(n, d)
    xv = x.reshape(FFT_N2, n1 * d)
    out = pl.pallas_call(
        _fft_stage2_kernel,
        grid=(n1,),
        in_specs=[
            pl.BlockSpec((FFT_N2, d), lambda k1: (k1, 0)),
            pl.BlockSpec((FFT_N2, d), lambda k1: (k1, 0)),
            pl.BlockSpec((1, 2 * FFT_N2, 2 * FFT_N2), lambda k1: (k1, 0, 0)),
            _const_spec(tabs["cc"].shape), _const_spec(tabs["sc"].shape), _const_spec(w.shape),
            pl.BlockSpec((FFT_N2, d), lambda k1: (0, k1)),
            _const_spec(mod.shape),
        ],
        out_specs=pl.BlockSpec((FFT_N2, d), lambda k1: (0, k1)),
        out_shape=jax.ShapeDtypeStruct((FFT_N2, n1 * d), F32),
        compiler_params=_params("arbitrary"),
        name="fft_stage2",
    )(yr, yi, tabs["m2"], tabs["cc"], tabs["sc"], w, xv, mod)
    return out.reshape(n, d)


def _fourier_short(x, mod, g1, w, tabs):
    n, d = x.shape
    return pl.pallas_call(
        _dft_small_kernel,
        grid=(1,),
        in_specs=[_const_spec(a.shape) for a in (x, mod, g1, tabs["m"], tabs["cc"], tabs["sc"], w)],
        out_specs=pl.BlockSpec((n, d), lambda i: (0, 0)),
        out_shape=jax.ShapeDtypeStruct((n, d), F32),
        compiler_params=_params("arbitrary"),
        name="dft_ctx",
    )(x, mod, g1, tabs["m"], tabs["cc"], tabs["sc"], w)


def _fourier(x, mod, g1, w, b2=4):
    n = x.shape[0]
    tabs = _dft_tables(n)
    if "m2" in tabs:
        return _fourier_long(x, mod, g1, w, tabs, b2)
    return _fourier_short(x, mod, g1, w, tabs)


def _rot_cols(w):
    q = QK_ROPE // 4
    a1, a2, b1, b2 = w[..., :q], w[..., q:2 * q], w[..., 2 * q:3 * q], w[..., 3 * q:]
    return jnp.concatenate([-a2, a1, -b2, b1], axis=-1)


def _pad_lanes(w):
    return jnp.pad(w, [(0, 0)] * (w.ndim - 1) + [(0, LANES - w.shape[-1])])


def _mla_weights(w_dqkv, g_q, g_kv, w_uq, w_ukv):
    hd = N_HEADS * LANES
    kr = w_dqkv[:, Q_LORA + KV_LORA:]
    wd = jnp.concatenate([w_dqkv[:, :Q_LORA + KV_LORA], _pad_lanes(kr), _pad_lanes(_rot_cols(kr))],
                         axis=1).astype(BF16)
    q3 = w_uq.reshape(Q_LORA, N_HEADS, QK_NOPE + QK_ROPE)
    qr = q3[:, :, QK_NOPE:]
    wq = jnp.concatenate([q3[:, :, :QK_NOPE].reshape(Q_LORA, hd),
                          _pad_lanes(qr).reshape(Q_LORA, hd),
                          _pad_lanes(_rot_cols(qr)).reshape(Q_LORA, hd)], axis=1).astype(BF16)
    kv3 = w_ukv.reshape(KV_LORA, N_HEADS, QK_NOPE + V_DIM)
    wk = kv3[:, :, :QK_NOPE].reshape(KV_LORA, hd).astype(BF16)
    wvT = kv3[:, :, QK_NOPE:].reshape(KV_LORA, N_HEADS * V_DIM).T.astype(BF16)
    return wd, g_q.reshape(1, -1), g_kv.reshape(1, -1), wq, wk, wvT


def _rope_tables(n):
    rows = n // GRID_W
    r, col = jnp.meshgrid(jnp.arange(rows, dtype=F32), jnp.arange(GRID_W, dtype=F32), indexing="ij")
    half = QK_ROPE // 2
    inv_freq = jnp.power(ROPE_BASE, -jnp.arange(0, half, 2, dtype=F32) / half)
    ang_r = r.reshape(-1)[:, None] * inv_freq
    ang_c = col.reshape(-1)[:, None] * inv_freq
    ang = jnp.concatenate([ang_r, ang_r, ang_c, ang_c], axis=-1)
    return _pad_lanes(jnp.cos(ang)), _pad_lanes(jnp.sin(ang))


def _row_tile(n, pref):
    return pref if n % pref == 0 else n


def kernel(x, c, ctx, c_ctx, ada_w, ada_b, norm1_g, norm2_g, mla_w_dqkv, mla_q_norm_g, mla_kv_norm_g,
           mla_w_uq, mla_w_ukv, mla_w_o, fno_w, ffn_w_up, ffn_conv_w, ffn_conv_b, ffn_w_down,
           final_norm_g):
    assert x.shape[0] == 1 and c.shape[0] == 1 and ctx.shape[0] == 1
    xs = x[0]
    cs = ctx[0]
    n, nc = xs.shape[0], cs.shape[0]
    mods = _ada_all(jnp.concatenate([c, c_ctx[None, :]], axis=0), ada_w, ada_b)
    cos, sin = _rope_tables(n)
    ones_c = _pad_lanes(jnp.ones((nc, QK_ROPE), F32))
    zeros_c = jnp.zeros((nc, LANES), F32)
    fin_g = final_norm_g.reshape(1, -1)

    for i in range(DEPTH):
        kind = i % N_MIXERS
        j = i // N_MIXERS
        ctx_later = any(l % N_MIXERS == MIXER_MLA for l in range(i + 1, DEPTH))
        mod_x, mod_c = mods[i, 0], mods[i, 1]
        g1 = norm1_g[i].reshape(1, -1)
        g2 = norm2_g[i].reshape(1, -1)
        if kind == MIXER_MLA:
            wts = _mla_weights(mla_w_dqkv[j], mla_q_norm_g[j], mla_kv_norm_g[j], mla_w_uq[j], mla_w_ukv[j])
            w_o = mla_w_o[j].astype(BF16)
            qx, kx, vTx = _mla_proj(xs, mod_x, g1, wts, cos, sin, _row_tile(n, 256))
            qc, kc, vTc = _mla_proj(cs, mod_c, g1, wts, ones_c, zeros_c, _row_tile(nc, 256))
            ox = _flash(qx, [(kc, vTc), (kx, vTx)], _row_tile(n, 512), 512)
            xs = _outproj(ox, w_o, xs, mod_x, _row_tile(n, 512))
            if ctx_later:
                oc = _flash(qc, [(kc, vTc)], _row_tile(nc, 256), 512)
                cs = _outproj(oc, w_o, cs, mod_c, _row_tile(nc, 256))
        else:
            w = fno_w[j].astype(BF16)
            xs = _fourier(xs, mod_x, g1, w)
            if ctx_later:
                cs = _fourier(cs, mod_c, g1, w)
        w_up = ffn_w_up[i].astype(BF16)
        w_down = ffn_w_down[i].astype(BF16)
        cb = ffn_conv_b[i].reshape(1, -1)
        xs = _ffn(xs, mod_x, g2, w_up, ffn_conv_w[i], cb, w_down, fin_g, _row_tile(n, 512), 512,
                  final=(i == DEPTH - 1))
        if ctx_later:
            cs = _ffn(cs, mod_c, g2, w_up, ffn_conv_w[i], cb, w_down, fin_g, _row_tile(nc, 256), 512,
                      final=False)
    return xs[None]
```

```python
import functools
import math

import jax
import jax.numpy as jnp
import numpy as np
from jax import lax
from jax.experimental import pallas as pl
from jax.experimental.pallas import tpu as pltpu

D_MODEL = 2048
DEPTH = 4
GRID_W = 64
N_MIXERS = 2
MIXER_MLA = 0
N_HEADS = 16
Q_LORA = 512
KV_LORA = 512
QK_NOPE = 128
QK_ROPE = 64
V_DIM = 128
ROPE_BASE = 10000.0
SM_SCALE = (QK_NOPE + QK_ROPE) ** -0.5
FNO_GROUPS = 8
FNO_GROUP_DIM = D_MODEL // FNO_GROUPS
D_FF = 5632
NORM_EPS = 1e-6
N_MOD = 6

LANES = 128
SUBLANES = 8
BF16_ROWS = 16
FLASH_UNROLL = 8
VMEM_LIMIT_BYTES = 56 * 1024 * 1024

HEAD_PAD = 2 * LANES
VT_ROWS = V_DIM + SUBLANES
Q_SCALE = SM_SCALE * math.log2(math.e)
FFT_N2 = 128
FFT_B2 = SUBLANES

BF16 = jnp.bfloat16
F32 = jnp.float32
HIGHEST = lax.Precision.HIGHEST

SH1, SC1, G1, SH2, SC2, G2 = range(6)


def _params(*sem):
    return pltpu.CompilerParams(dimension_semantics=sem, vmem_limit_bytes=VMEM_LIMIT_BYTES)


def _const_spec(shape):
    nd = len(shape)
    return pl.BlockSpec(shape, lambda *_: (0,) * nd, pipeline_mode=pl.Buffered(1))


def _dot(a, b):
    return jnp.dot(a, b, preferred_element_type=F32)


def _dot_hi(a, b):
    return jnp.dot(a, b, preferred_element_type=F32, precision=HIGHEST)


def _rms(x, g):
    return x * lax.rsqrt(jnp.mean(x * x, axis=-1, keepdims=True) + NORM_EPS) * g


def _norm_mod(x, g, shift, scale):
    return _rms(x, g) * (1.0 + scale) + shift


def _ada_kernel(condT_ref, w_ref, b_ref, o_ref, acc_ref):
    k = pl.program_id(2)
    tk, tn = w_ref.shape[1], w_ref.shape[2]
    n_rows = acc_ref.shape[0]

    @pl.when(k == 0)
    def _():
        acc_ref[...] = jnp.zeros_like(acc_ref)

    w = w_ref[0]
    ct = condT_ref[pl.ds(pl.multiple_of(k * tk, tk), tk), :]
    s = ct * (1.0 / (1.0 + jnp.exp(-ct)))
    for r in range(n_rows):
        prod = w * s[:, r:r + 1]
        acc_ref[r] += prod.reshape(tk // SUBLANES, SUBLANES, tn).sum(axis=0)

    @pl.when(k == pl.num_programs(2) - 1)
    def _():
        for r in range(n_rows):
            o_ref[0, r:r + 1, :] = acc_ref[r].sum(axis=0, keepdims=True) + b_ref[0]


def _ada_all(cond, ada_w, ada_b):
    n_rows, d = cond.shape
    depth, _, nout = ada_w.shape
    tk, tn = 512, 2048
    out = pl.pallas_call(
        _ada_kernel,
        grid=(depth, nout // tn, d // tk),
        in_specs=[
            pl.BlockSpec((d, n_rows), lambda l, j, k: (0, 0)),
            pl.BlockSpec((1, tk, tn), lambda l, j, k: (l, k, j)),
            pl.BlockSpec((1, 1, tn), lambda l, j, k: (l, 0, j)),
        ],
        out_specs=pl.BlockSpec((1, n_rows, tn), lambda l, j, k: (l, 0, j)),
        out_shape=jax.ShapeDtypeStruct((depth, n_rows, nout), F32),
        scratch_shapes=[pltpu.VMEM((n_rows, SUBLANES, tn), F32)],
        compiler_params=_params("arbitrary", "arbitrary", "arbitrary"),
        name="ada_mod",
    )(cond.T, ada_w, ada_b.reshape(depth, 1, nout))
    return out.reshape(depth, n_rows, N_MOD, d)


def _mla_proj_kernel(x_ref, mod_ref, g1_ref, wd_ref, gq_ref, gkv_ref, wq_ref, wk_ref, wvT_ref,
                     cos_ref, sin_ref, q_ref, k_ref, vT_ref):
    hd = N_HEADS * LANES
    h = _norm_mod(x_ref[...], g1_ref[...], mod_ref[SH1:SH1 + 1], mod_ref[SC1:SC1 + 1])
    d = _dot(h.astype(BF16), wd_ref[...])
    cq = _rms(d[:, :Q_LORA], gq_ref[...]).astype(BF16)
    ckv = _rms(d[:, Q_LORA:Q_LORA + KV_LORA], gkv_ref[...]).astype(BF16)
    cos = cos_ref[...]
    sin = sin_ref[...]
    base = Q_LORA + KV_LORA
    kr = (d[:, base:base + LANES] * cos + d[:, base + LANES:base + 2 * LANES] * sin).astype(BF16)
    qall = _dot(cq, wq_ref[...])
    kn = _dot(ckv, wk_ref[...])
    vT = lax.dot_general(wvT_ref[...], ckv, (((1,), (1,)), ((), ())),
                         preferred_element_type=F32)
    tm = x_ref.shape[0]
    ones_row = (lax.broadcasted_iota(jnp.int32, (VT_ROWS - V_DIM, tm), 0) == 0).astype(BF16)
    for hh in range(N_HEADS):
        lo, hi = hh * LANES, (hh + 1) * LANES
        qr = qall[:, hd + lo:hd + hi] * cos + qall[:, 2 * hd + lo:2 * hd + hi] * sin
        q_ref[hh, :, 0:LANES] = (qall[:, lo:hi] * Q_SCALE).astype(BF16)
        q_ref[hh, :, LANES:HEAD_PAD] = (qr * Q_SCALE).astype(BF16)
        k_ref[hh, :, 0:LANES] = kn[:, lo:hi].astype(BF16)
        k_ref[hh, :, LANES:HEAD_PAD] = kr
        vT_ref[hh, 0:V_DIM, :] = vT[lo:hi, :].astype(BF16)
        vT_ref[hh, V_DIM:VT_ROWS, :] = ones_row


def _mla_proj(x, mod, g1, wts, cos, sin, tm):
    n, d = x.shape
    wd, gq, gkv, wq, wk, wvT = wts
    return pl.pallas_call(
        _mla_proj_kernel,
        grid=(n // tm,),
        in_specs=[
            pl.BlockSpec((tm, d), lambda i: (i, 0)),
            _const_spec(mod.shape), _const_spec(g1.shape), _const_spec(wd.shape),
            _const_spec(gq.shape), _const_spec(gkv.shape), _const_spec(wq.shape),
            _const_spec(wk.shape), _const_spec(wvT.shape),
            pl.BlockSpec((tm, LANES), lambda i: (i, 0)),
            pl.BlockSpec((tm, LANES), lambda i: (i, 0)),
        ],
        out_specs=[
            pl.BlockSpec((N_HEADS, tm, HEAD_PAD), lambda i: (0, i, 0)),
            pl.BlockSpec((N_HEADS, tm, HEAD_PAD), lambda i: (0, i, 0)),
            pl.BlockSpec((N_HEADS, VT_ROWS, tm), lambda i: (0, 0, i)),
        ],
        out_shape=[
            jax.ShapeDtypeStruct((N_HEADS, n, HEAD_PAD), BF16),
            jax.ShapeDtypeStruct((N_HEADS, n, HEAD_PAD), BF16),
            jax.ShapeDtypeStruct((N_HEADS, VT_ROWS, n), BF16),
        ],
        compiler_params=_params("arbitrary"),
        name="mla_proj",
    )(x, mod, g1, wd, gq, gkv, wq, wk, wvT, cos, sin)


def _flash_kernel(*refs, seg_lens, tk, unroll):
    nseg = len(seg_lens)
    q_ref = refs[0]
    kv_refs = refs[1:1 + 2 * nseg]
    o_ref = refs[1 + 2 * nseg]
    m_ref, acc_ref, s0_ref, s1_ref = refs[2 + 2 * nseg:]
    s_refs = (s0_ref, s1_ref)
    q = q_ref[0]
    m_ref[...] = jnp.full_like(m_ref, -jnp.inf)
    acc_ref[...] = jnp.zeros_like(acc_ref)

    def scores(k_ref, c, ck):
        k = k_ref[0, pl.ds(pl.multiple_of(c * ck, ck), ck), :]
        return lax.dot_general(k, q, (((1,), (1,)), ((), ())), preferred_element_type=F32)

    def update(s, vT_ref, c, ck):
        m_old = m_ref[...]
        m_new = jnp.maximum(m_old, jnp.max(s, axis=0, keepdims=True))
        p = jnp.exp2(s - m_new).astype(BF16)
        vT = vT_ref[0, :, pl.ds(pl.multiple_of(c * ck, ck), ck)]
        acc_ref[...] = jnp.exp2(m_old - m_new) * acc_ref[...] + _dot(vT, p)
        m_ref[...] = m_new

    for si in range(nseg):
        k_ref, vT_ref = kv_refs[2 * si], kv_refs[2 * si + 1]
        ck = min(tk, seg_lens[si])
        nch = seg_lens[si] // ck
        if nch % unroll:
            for c in range(nch):
                update(scores(k_ref, c, ck), vT_ref, c, ck)
            continue

        assert ck == s0_ref.shape[0] and unroll % 2 == 0
        s0_ref[...] = scores(k_ref, 0, ck)

        def body(t, carry, k_ref=k_ref, vT_ref=vT_ref, ck=ck, nch=nch):
            for u in range(unroll):
                c = t * unroll + u
                s_refs[(u + 1) % 2][...] = scores(k_ref, jnp.minimum(c + 1, nch - 1), ck)
                update(s_refs[u % 2][...], vT_ref, c, ck)
            return carry

        lax.fori_loop(0, nch // unroll, body, 0)

    acc = acc_ref[...]
    o = acc[:V_DIM] * (1.0 / acc[V_DIM:V_DIM + 1])
    o_ref[...] = o.T.astype(o_ref.dtype)


def _flash(q, kvs, tq, tk):
    nh, n, _ = q.shape
    seg_lens = tuple(k.shape[1] for k, _ in kvs)
    in_specs = [pl.BlockSpec((1, tq, HEAD_PAD), lambda h, i: (h, i, 0))]
    args = [q]
    for k, vT in kvs:
        nk = k.shape[1]
        in_specs.append(pl.BlockSpec((1, nk, HEAD_PAD), lambda h, i: (h, 0, 0)))
        in_specs.append(pl.BlockSpec((1, VT_ROWS, nk), lambda h, i: (h, 0, 0)))
        args += [k, vT]
    ck = min(tk, max(seg_lens))
    return pl.pallas_call(
        functools.partial(_flash_kernel, seg_lens=seg_lens, tk=tk, unroll=FLASH_UNROLL),
        grid=(nh, n // tq),
        in_specs=in_specs,
        out_specs=pl.BlockSpec((tq, V_DIM), lambda h, i: (i, h)),
        out_shape=jax.ShapeDtypeStruct((n, nh * V_DIM), BF16),
        scratch_shapes=[pltpu.VMEM((1, tq), F32), pltpu.VMEM((VT_ROWS, tq), F32),
                        pltpu.VMEM((ck, tq), F32), pltpu.VMEM((ck, tq), F32)],
        compiler_params=_params("arbitrary", "arbitrary"),
        name="mla_flash",
    )(*args)


def _outproj_kernel(o_ref, w_ref, x_ref, mod_ref, out_ref):
    out_ref[...] = x_ref[...] + mod_ref[G1:G1 + 1] * _dot(o_ref[...], w_ref[...])


def _outproj(o, w, x, mod, tm):
    n, d = x.shape
    return pl.pallas_call(
        _outproj_kernel,
        grid=(n // tm,),
        in_specs=[
            pl.BlockSpec((tm, o.shape[1]), lambda i: (i, 0)),
            _const_spec(w.shape),
            pl.BlockSpec((tm, d), lambda i: (i, 0)),
            _const_spec(mod.shape),
        ],
        out_specs=pl.BlockSpec((tm, d), lambda i: (i, 0)),
        out_shape=jax.ShapeDtypeStruct((n, d), F32),
        compiler_params=_params("arbitrary"),
        name="mla_outproj",
    )(o, w, x, mod)


def _ffn_kernel(xp_ref, x_ref, xn_ref, mod_ref, g2_ref, wuv_ref, wug_ref, cwv_ref, cwg_ref,
                cbv_ref, cbg_ref, wd_ref, fg_ref, o_ref, h_ref, acc_ref, *, tm, final):
    i = pl.program_id(0)
    f = pl.program_id(1)
    halo = BF16_ROWS
    rows = tm + 2 * halo

    @pl.when(f == 0)
    def _():
        g, sh, sc = g2_ref[...], mod_ref[SH2:SH2 + 1], mod_ref[SC2:SC2 + 1]
        hp = jnp.where(i > 0, _norm_mod(xp_ref[...], g, sh, sc), 0.0)
        hn = jnp.where(i < pl.num_programs(0) - 1, _norm_mod(xn_ref[...], g, sh, sc), 0.0)
        h_ref[0:halo] = hp.astype(BF16)
        h_ref[halo:halo + tm] = _norm_mod(x_ref[...], g, sh, sc).astype(BF16)
        h_ref[halo + tm:rows] = hn.astype(BF16)
        acc_ref[...] = jnp.zeros_like(acc_ref)

    hh = h_ref[...]

    def conv(w_ref, cw_ref, cb_ref):
        u = _dot(hh, w_ref[...])
        cw = cw_ref[...]
        r = (pltpu.roll(u, 1, 0) * cw[0:1] + u * cw[1:2] + pltpu.roll(u, rows - 1, 0) * cw[2:3]
             + cb_ref[...])
        return r[halo:halo + tm]

    val = conv(wuv_ref, cwv_ref, cbv_ref)
    gate = conv(wug_ref, cwg_ref, cbg_ref)
    act = val * (gate * (1.0 / (1.0 + jnp.exp(-gate))))
    acc_ref[...] += _dot(act.astype(BF16), wd_ref[...])

    @pl.when(f == pl.num_programs(1) - 1)
    def _():
        out = x_ref[...] + mod_ref[G2:G2 + 1] * acc_ref[...]
        if final:
            out = _rms(out, fg_ref[...])
        o_ref[...] = out


def _ffn(x, mod, g2, layer, w_up, conv_w, conv_b, w_down, final_g, tm, tf, final):
    n, d = x.shape
    dff = w_down.shape[1]
    nf = dff // tf
    halo = BF16_ROWS
    hb = tm // halo
    last = n // halo - 1
    return pl.pallas_call(
        functools.partial(_ffn_kernel, tm=tm, final=final),
        grid=(n // tm, nf),
        in_specs=[
            pl.BlockSpec((halo, d), lambda i, f: (jnp.maximum(i * hb - 1, 0), 0)),
            pl.BlockSpec((tm, d), lambda i, f: (i, 0)),
            pl.BlockSpec((halo, d), lambda i, f: (jnp.minimum((i + 1) * hb, last), 0)),
            _const_spec(mod.shape), _const_spec(g2.shape),
            pl.BlockSpec((None, d, tf), lambda i, f: (layer, 0, f)),
            pl.BlockSpec((None, d, tf), lambda i, f: (layer, 0, nf + f)),
            pl.BlockSpec((None, 3, tf), lambda i, f: (layer, 0, f)),
            pl.BlockSpec((None, 3, tf), lambda i, f: (layer, 0, nf + f)),
            pl.BlockSpec((None, 1, tf), lambda i, f: (layer, 0, f)),
            pl.BlockSpec((None, 1, tf), lambda i, f: (layer, 0, nf + f)),
            pl.BlockSpec((None, tf, d), lambda i, f: (layer, f, 0)),
            _const_spec(final_g.shape),
        ],
        out_specs=pl.BlockSpec((tm, d), lambda i, f: (i, 0)),
        out_shape=jax.ShapeDtypeStruct((n, d), F32),
        scratch_shapes=[pltpu.VMEM((tm + 2 * halo, d), BF16), pltpu.VMEM((tm, d), F32)],
        compiler_params=_params("arbitrary", "arbitrary"),
        name="conv_ffn",
    )(x, x, x, mod, g2, w_up, w_up, conv_w, conv_w, conv_b, conv_b, w_down, final_g)


def _channel_dft(xr, xi, cc_ref, sc_ref, dot):
    parts = []
    for g in range(FNO_GROUPS):
        lo, hi = g * FNO_GROUP_DIM, (g + 1) * FNO_GROUP_DIM
        parts.append(dot(xr[:, lo:hi], cc_ref[...]) + dot(xi[:, lo:hi], sc_ref[...]))
    return jnp.concatenate(parts, axis=1)


def _fft_stage1_kernel(x_ref, mod_ref, g1_ref, kc_ref, ks_ref, yr_ref, yi_ref):
    n1, b2, d = x_ref.shape
    x = x_ref[...].reshape(n1 * b2, d)
    h = _norm_mod(x, g1_ref[...], mod_ref[SH1:SH1 + 1], mod_ref[SC1:SC1 + 1]).astype(BF16)
    yr_ref[...] = _dot(kc_ref[...], h).reshape(n1, b2, d)
    yi_ref[...] = _dot(ks_ref[...], h).reshape(n1, b2, d)


def _fft_stage2_kernel(yr_ref, yi_ref, m_ref, cc_ref, sc_ref, p_ref, w_ref, x_ref, mod_ref, o_ref,
                       f_ref, fp_ref):
    b2 = m_ref.shape[0]
    tn = o_ref.shape[2]

    @pl.when(pl.program_id(1) == 0)
    def _():
        for q in range(b2):
            rows = slice(q * FFT_N2, (q + 1) * FFT_N2)
            ys = jnp.concatenate([yr_ref[rows], yi_ref[rows]], axis=0).astype(BF16)
            xs = _dot(m_ref[q], ys).astype(BF16)
            f_ref[rows] = _channel_dft(xs[:FFT_N2], xs[FFT_N2:], cc_ref, sc_ref, _dot).astype(BF16)
        fp_ref[...] = _dot(p_ref[...], f_ref[...]).astype(BF16)

    y = _dot(fp_ref[...], w_ref[...])
    o_ref[...] = x_ref[...] + mod_ref[G1:G1 + 1].reshape(1, 1, tn) * y.reshape(FFT_N2, b2, tn)


def _dft_small_kernel(x_ref, mod_ref, g1_ref, m_ref, cc_ref, sc_ref, w_ref, o_ref):
    n = x_ref.shape[0]
    x = x_ref[...]
    h = _norm_mod(x, g1_ref[...], mod_ref[SH1:SH1 + 1], mod_ref[SC1:SC1 + 1])
    xs = _dot_hi(m_ref[...], h)
    fmix = _channel_dft(xs[:n], xs[n:], cc_ref, sc_ref, _dot_hi).astype(BF16)
    o_ref[...] = x + mod_ref[G1:G1 + 1] * _dot(fmix, w_ref[...])


def _cos_sin(num, den, scale):
    ang = (2.0 * np.pi / den) * (num % den).astype(np.float64)
    return (np.cos(ang) * scale).astype(np.float32), (np.sin(ang) * scale).astype(np.float32)


def _dft_tables(n):
    gd = FNO_GROUP_DIM
    a = np.arange(gd, dtype=np.int64)
    cc, sc = _cos_sin(a[:, None] * a[None, :], gd, gd ** -0.5)
    if n % (FFT_N2 * FFT_B2) == 0 and n > 2 * FFT_N2:
        n1 = n // FFT_N2
        j1 = np.arange(n1, dtype=np.int64)
        c1, s1 = _cos_sin(j1[:, None] * j1[None, :], n1, n1 ** -0.5)
        eye = np.eye(FFT_B2, dtype=np.float32)
        k = j1[:, None, None] + n1 * np.arange(FFT_N2, dtype=np.int64)[None, :, None]
        j2 = np.arange(FFT_N2, dtype=np.int64)[None, None, :]
        mc, ms = _cos_sin(k * j2, n, FFT_N2 ** -0.5)
        m2 = np.concatenate([np.concatenate([mc, ms], axis=2),
                             np.concatenate([-ms, mc], axis=2)], axis=1)
        rows = np.arange(FFT_N2 * FFT_B2)
        perm = np.zeros((FFT_N2 * FFT_B2,) * 2, np.float32)
        perm[rows, (rows % FFT_B2) * FFT_N2 + rows // FFT_B2] = 1.0
        tabs = {"kc": np.kron(c1, eye), "ks": np.kron(-s1, eye), "m2": m2, "perm": perm,
                "cc": cc, "sc": sc}
        return {k_: jnp.asarray(v).astype(BF16) for k_, v in tabs.items()}
    j = np.arange(n, dtype=np.int64)
    c, s = _cos_sin(j[:, None] * j[None, :], n, n ** -0.5)
    tabs = {"m": np.concatenate([c, -s], axis=0), "cc": cc, "sc": sc}
    return {k_: jnp.asarray(v) for k_, v in tabs.items()}


def _fourier_long(x, mod, g1, layer, w, tabs, tn):
    n, d = x.shape
    n1 = n // FFT_N2
    nb = n1 * FFT_B2
    yr, yi = pl.pallas_call(
        _fft_stage1_kernel,
        grid=(FFT_N2 // FFT_B2,),
        in_specs=[
            pl.BlockSpec((n1, FFT_B2, d), lambda s: (0, s, 0)),
            _const_spec(mod.shape), _const_spec(g1.shape),
            _const_spec((nb, nb)), _const_spec((nb, nb)),
        ],
        out_specs=[pl.BlockSpec((n1, FFT_B2, d), lambda s: (0, s, 0))] * 2,
        out_shape=[jax.ShapeDtypeStruct((n1, FFT_N2, d), F32)] * 2,
        compiler_params=_params("arbitrary"),
        name="fft_stage1",
    )(x.reshape(n1, FFT_N2, d), mod, g1, tabs["kc"], tabs["ks"])
    rows = FFT_N2 * FFT_B2
    out = pl.pallas_call(
        _fft_stage2_kernel,
        grid=(n1 // FFT_B2, d // tn),
        in_specs=[
            pl.BlockSpec((rows, d), lambda a, j: (a, 0), pipeline_mode=pl.Buffered(1)),
            pl.BlockSpec((rows, d), lambda a, j: (a, 0), pipeline_mode=pl.Buffered(1)),
            pl.BlockSpec((FFT_B2, 2 * FFT_N2, 2 * FFT_N2), lambda a, j: (a, 0, 0)),
            _const_spec(tabs["cc"].shape), _const_spec(tabs["sc"].shape),
            _const_spec(tabs["perm"].shape),
            pl.BlockSpec((None, d, tn), lambda a, j: (layer, 0, j)),
            pl.BlockSpec((FFT_N2, FFT_B2, tn), lambda a, j: (0, a, j)),
            pl.BlockSpec((N_MOD, tn), lambda a, j: (0, j)),
        ],
        out_specs=pl.BlockSpec((FFT_N2, FFT_B2, tn), lambda a, j: (0, a, j)),
        out_shape=jax.ShapeDtypeStruct((FFT_N2, n1, d), F32),
        scratch_shapes=[pltpu.VMEM((rows, d), BF16), pltpu.VMEM((rows, d), BF16)],
        compiler_params=_params("arbitrary", "arbitrary"),
        name="fft_stage2",
    )(yr.reshape(n, d), yi.reshape(n, d), tabs["m2"], tabs["cc"], tabs["sc"], tabs["perm"], w,
      x.reshape(FFT_N2, n1, d), mod)
    return out.reshape(n, d)


def _fourier_short(x, mod, g1, layer, w, tabs):
    n, d = x.shape
    return pl.pallas_call(
        _dft_small_kernel,
        grid=(1,),
        in_specs=[_const_spec(a.shape) for a in (x, mod, g1, tabs["m"], tabs["cc"], tabs["sc"])]
        + [pl.BlockSpec((None, d, d), lambda i: (layer, 0, 0))],
        out_specs=pl.BlockSpec((n, d), lambda i: (0, 0)),
        out_shape=jax.ShapeDtypeStruct((n, d), F32),
        compiler_params=_params("arbitrary"),
        name="dft_ctx",
    )(x, mod, g1, tabs["m"], tabs["cc"], tabs["sc"], w)


def _fourier(x, mod, g1, layer, w):
    tabs = _dft_tables(x.shape[0])
    if "m2" in tabs:
        return _fourier_long(x, mod, g1, layer, w, tabs, 512)
    return _fourier_short(x, mod, g1, layer, w, tabs)


def _rot_cols(w):
    q = QK_ROPE // 4
    a1, a2, b1, b2 = w[..., :q], w[..., q:2 * q], w[..., 2 * q:3 * q], w[..., 3 * q:]
    return jnp.concatenate([-a2, a1, -b2, b1], axis=-1)


def _pad_lanes(w):
    return jnp.pad(w, [(0, 0)] * (w.ndim - 1) + [(0, LANES - w.shape[-1])])


def _mla_weights(w_dqkv, g_q, g_kv, w_uq, w_ukv):
    hd = N_HEADS * LANES
    kr = w_dqkv[:, Q_LORA + KV_LORA:]
    wd = jnp.concatenate([w_dqkv[:, :Q_LORA + KV_LORA], _pad_lanes(kr), _pad_lanes(_rot_cols(kr))],
                         axis=1).astype(BF16)
    q3 = w_uq.reshape(Q_LORA, N_HEADS, QK_NOPE + QK_ROPE)
    qr = q3[:, :, QK_NOPE:]
    wq = jnp.concatenate([q3[:, :, :QK_NOPE].reshape(Q_LORA, hd),
                          _pad_lanes(qr).reshape(Q_LORA, hd),
                          _pad_lanes(_rot_cols(qr)).reshape(Q_LORA, hd)], axis=1).astype(BF16)
    kv3 = w_ukv.reshape(KV_LORA, N_HEADS, QK_NOPE + V_DIM)
    wk = kv3[:, :, :QK_NOPE].reshape(KV_LORA, hd).astype(BF16)
    wvT = kv3[:, :, QK_NOPE:].reshape(KV_LORA, N_HEADS * V_DIM).T.astype(BF16)
    return wd, g_q.reshape(1, -1), g_kv.reshape(1, -1), wq, wk, wvT


def _rope_tables(n):
    rows = n // GRID_W
    r, col = jnp.meshgrid(jnp.arange(rows, dtype=F32), jnp.arange(GRID_W, dtype=F32), indexing="ij")
    half = QK_ROPE // 2
    inv_freq = jnp.power(ROPE_BASE, -jnp.arange(0, half, 2, dtype=F32) / half)
    ang_r = r.reshape(-1)[:, None] * inv_freq
    ang_c = col.reshape(-1)[:, None] * inv_freq
    ang = jnp.concatenate([ang_r, ang_r, ang_c, ang_c], axis=-1)
    return _pad_lanes(jnp.cos(ang)), _pad_lanes(jnp.sin(ang))


def _row_tile(n, pref):
    return pref if n % pref == 0 else n


def kernel(x, c, ctx, c_ctx, ada_w, ada_b, norm1_g, norm2_g, mla_w_dqkv, mla_q_norm_g, mla_kv_norm_g,
           mla_w_uq, mla_w_ukv, mla_w_o, fno_w, ffn_w_up, ffn_conv_w, ffn_conv_b, ffn_w_down,
           final_norm_g):
    assert x.shape[0] == 1 and c.shape[0] == 1 and ctx.shape[0] == 1
    xs = x[0]
    cs = ctx[0]
    n, nc = xs.shape[0], cs.shape[0]
    mods = _ada_all(jnp.concatenate([c, c_ctx[None, :]], axis=0), ada_w, ada_b)
    cos, sin = _rope_tables(n)
    ones_c = _pad_lanes(jnp.ones((nc, QK_ROPE), F32))
    zeros_c = jnp.zeros((nc, LANES), F32)
    fin_g = final_norm_g.reshape(1, -1)
    w_up = ffn_w_up.astype(BF16)
    w_down = ffn_w_down.astype(BF16)
    w_fno = fno_w.astype(BF16)
    conv_b = ffn_conv_b.reshape(DEPTH, 1, -1)

    for i in range(DEPTH):
        kind = i % N_MIXERS
        j = i // N_MIXERS
        ctx_later = any(l % N_MIXERS == MIXER_MLA for l in range(i + 1, DEPTH))
        mod_x, mod_c = mods[i, 0], mods[i, 1]
        g1 = norm1_g[i].reshape(1, -1)
        g2 = norm2_g[i].reshape(1, -1)
        if kind == MIXER_MLA:
            wts = _mla_weights(mla_w_dqkv[j], mla_q_norm_g[j], mla_kv_norm_g[j], mla_w_uq[j], mla_w_ukv[j])
            w_o = mla_w_o[j].astype(BF16)
            qx, kx, vTx = _mla_proj(xs, mod_x, g1, wts, cos, sin, _row_tile(n, 256))
            qc, kc, vTc = _mla_proj(cs, mod_c, g1, wts, ones_c, zeros_c, _row_tile(nc, 256))
            ox = _flash(qx, [(kc, vTc), (kx, vTx)], _row_tile(n, 512), 512)
            xs = _outproj(ox, w_o, xs, mod_x, _row_tile(n, 512))
            if ctx_later:
                oc = _flash(qc, [(kc, vTc)], _row_tile(nc, 256), 512)
                cs = _outproj(oc, w_o, cs, mod_c, _row_tile(nc, 256))
        else:
            xs = _fourier(xs, mod_x, g1, j, w_fno)
            if ctx_later:
                cs = _fourier(cs, mod_c, g1, j, w_fno)
        xs = _ffn(xs, mod_x, g2, i, w_up, ffn_conv_w, conv_b, w_down, fin_g, _row_tile(n, 512), 512,
                  final=(i == DEPTH - 1))
        if ctx_later:
            cs = _ffn(cs, mod_c, g2, i, w_up, ffn_conv_w, conv_b, w_down, fin_g, _row_tile(nc, 256), 512,
                      final=False)
    return xs[None]
```

```python
import functools
import math

import jax
import jax.numpy as jnp
import numpy as np
from jax import lax
from jax.experimental import pallas as pl
from jax.experimental.pallas import tpu as pltpu

D_MODEL = 2048
DEPTH = 4
GRID_W = 64
N_MIXERS = 2
MIXER_MLA = 0
N_HEADS = 16
Q_LORA = 512
KV_LORA = 512
QK_NOPE = 128
QK_ROPE = 64
V_DIM = 128
ROPE_BASE = 10000.0
SM_SCALE = (QK_NOPE + QK_ROPE) ** -0.5
FNO_GROUPS = 8
FNO_GROUP_DIM = D_MODEL // FNO_GROUPS
D_FF = 5632
NORM_EPS = 1e-6
N_MOD = 6

LANES = 128
SUBLANES = 8
BF16_ROWS = 16
FLASH_UNROLL = 8
NORM_ROWS = BF16_ROWS
FFN_COLS = 256
VMEM_LIMIT_BYTES = 56 * 1024 * 1024

HEAD_PAD = 2 * LANES
VT_ROWS = V_DIM + SUBLANES
Q_SCALE = SM_SCALE * math.log2(math.e)
FFT_N2 = 128
FFT_B2 = SUBLANES

BF16 = jnp.bfloat16
F32 = jnp.float32
HIGHEST = lax.Precision.HIGHEST

SH1, SC1, G1, SH2, SC2, G2 = range(6)


def _params(*sem):
    return pltpu.CompilerParams(dimension_semantics=sem, vmem_limit_bytes=VMEM_LIMIT_BYTES)


def _const_spec(shape):
    nd = len(shape)
    return pl.BlockSpec(shape, lambda *_: (0,) * nd, pipeline_mode=pl.Buffered(1))


def _dot(a, b):
    return jnp.dot(a, b, preferred_element_type=F32)


def _dot_hi(a, b):
    return jnp.dot(a, b, preferred_element_type=F32, precision=HIGHEST)


def _rms(x, g):
    return x * lax.rsqrt(jnp.mean(x * x, axis=-1, keepdims=True) + NORM_EPS) * g


def _norm_mod(x, g, shift, scale):
    return _rms(x, g) * (1.0 + scale) + shift


def _ada_kernel(condT_ref, w_ref, b_ref, o_ref, acc_ref):
    k = pl.program_id(2)
    tk, tn = w_ref.shape[1], w_ref.shape[2]
    n_rows = acc_ref.shape[0]

    @pl.when(k == 0)
    def _():
        acc_ref[...] = jnp.zeros_like(acc_ref)

    w = w_ref[0]
    ct = condT_ref[pl.ds(pl.multiple_of(k * tk, tk), tk), :]
    s = ct * (1.0 / (1.0 + jnp.exp(-ct)))
    for r in range(n_rows):
        prod = w * s[:, r:r + 1]
        acc_ref[r] += prod.reshape(tk // SUBLANES, SUBLANES, tn).sum(axis=0)

    @pl.when(k == pl.num_programs(2) - 1)
    def _():
        for r in range(n_rows):
            o_ref[0, r:r + 1, :] = acc_ref[r].sum(axis=0, keepdims=True) + b_ref[0]


def _ada_all(cond, ada_w, ada_b):
    n_rows, d = cond.shape
    depth, _, nout = ada_w.shape
    tk, tn = 1024, 2048
    out = pl.pallas_call(
        _ada_kernel,
        grid=(depth, nout // tn, d // tk),
        in_specs=[
            pl.BlockSpec((d, n_rows), lambda l, j, k: (0, 0)),
            pl.BlockSpec((1, tk, tn), lambda l, j, k: (l, k, j)),
            pl.BlockSpec((1, 1, tn), lambda l, j, k: (l, 0, j)),
        ],
        out_specs=pl.BlockSpec((1, n_rows, tn), lambda l, j, k: (l, 0, j)),
        out_shape=jax.ShapeDtypeStruct((depth, n_rows, nout), F32),
        scratch_shapes=[pltpu.VMEM((n_rows, SUBLANES, tn), F32)],
        compiler_params=_params("arbitrary", "arbitrary", "arbitrary"),
        name="ada_mod",
    )(cond.T, ada_w, ada_b.reshape(depth, 1, nout))
    return out.reshape(depth, n_rows, N_MOD, d)


def _mla_proj_kernel(x_ref, mod_ref, g1_ref, wd_ref, gq_ref, gkv_ref, wq_ref, wk_ref, wvT_ref,
                     cos_ref, sin_ref, q_ref, k_ref, vT_ref):
    hd = N_HEADS * LANES
    h = _norm_mod(x_ref[...], g1_ref[...], mod_ref[SH1:SH1 + 1], mod_ref[SC1:SC1 + 1])
    d = _dot(h.astype(BF16), wd_ref[...])
    cq = _rms(d[:, :Q_LORA], gq_ref[...]).astype(BF16)
    ckv = _rms(d[:, Q_LORA:Q_LORA + KV_LORA], gkv_ref[...]).astype(BF16)
    cos = cos_ref[...]
    sin = sin_ref[...]
    base = Q_LORA + KV_LORA
    kr = (d[:, base:base + LANES] * cos + d[:, base + LANES:base + 2 * LANES] * sin).astype(BF16)
    qall = _dot(cq, wq_ref[...])
    kn = _dot(ckv, wk_ref[...])
    vT = lax.dot_general(wvT_ref[...], ckv, (((1,), (1,)), ((), ())),
                         preferred_element_type=F32)
    tm = x_ref.shape[0]
    ones_row = (lax.broadcasted_iota(jnp.int32, (VT_ROWS - V_DIM, tm), 0) == 0).astype(BF16)
    for hh in range(N_HEADS):
        lo, hi = hh * LANES, (hh + 1) * LANES
        qr = qall[:, hd + lo:hd + hi] * cos + qall[:, 2 * hd + lo:2 * hd + hi] * sin
        q_ref[hh, :, 0:LANES] = (qall[:, lo:hi] * Q_SCALE).astype(BF16)
        q_ref[hh, :, LANES:HEAD_PAD] = (qr * Q_SCALE).astype(BF16)
        k_ref[hh, :, 0:LANES] = kn[:, lo:hi].astype(BF16)
        k_ref[hh, :, LANES:HEAD_PAD] = kr
        vT_ref[hh, 0:V_DIM, :] = vT[lo:hi, :].astype(BF16)
        vT_ref[hh, V_DIM:VT_ROWS, :] = ones_row


def _mla_proj(x, mod, g1, wts, cos, sin, tm):
    n, d = x.shape
    wd, gq, gkv, wq, wk, wvT = wts
    return pl.pallas_call(
        _mla_proj_kernel,
        grid=(n // tm,),
        in_specs=[
            pl.BlockSpec((tm, d), lambda i: (i, 0)),
            _const_spec(mod.shape), _const_spec(g1.shape), _const_spec(wd.shape),
            _const_spec(gq.shape), _const_spec(gkv.shape), _const_spec(wq.shape),
            _const_spec(wk.shape), _const_spec(wvT.shape),
            pl.BlockSpec((tm, LANES), lambda i: (i, 0)),
            pl.BlockSpec((tm, LANES), lambda i: (i, 0)),
        ],
        out_specs=[
            pl.BlockSpec((N_HEADS, tm, HEAD_PAD), lambda i: (0, i, 0)),
            pl.BlockSpec((N_HEADS, tm, HEAD_PAD), lambda i: (0, i, 0)),
            pl.BlockSpec((N_HEADS, VT_ROWS, tm), lambda i: (0, 0, i)),
        ],
        out_shape=[
            jax.ShapeDtypeStruct((N_HEADS, n, HEAD_PAD), BF16),
            jax.ShapeDtypeStruct((N_HEADS, n, HEAD_PAD), BF16),
            jax.ShapeDtypeStruct((N_HEADS, VT_ROWS, n), BF16),
        ],
        compiler_params=_params("arbitrary"),
        name="mla_proj",
    )(x, mod, g1, wd, gq, gkv, wq, wk, wvT, cos, sin)


def _flash_kernel(*refs, seg_lens, tk, unroll):
    nseg = len(seg_lens)
    q_ref = refs[0]
    kv_refs = refs[1:1 + 2 * nseg]
    o_ref = refs[1 + 2 * nseg]
    m_ref, acc_ref, s0_ref, s1_ref = refs[2 + 2 * nseg:]
    s_refs = (s0_ref, s1_ref)
    q = q_ref[0]
    m_ref[...] = jnp.full_like(m_ref, -jnp.inf)
    acc_ref[...] = jnp.zeros_like(acc_ref)

    def scores(k_ref, c, ck):
        k = k_ref[0, pl.ds(pl.multiple_of(c * ck, ck), ck), :]
        return lax.dot_general(k, q, (((1,), (1,)), ((), ())), preferred_element_type=F32)

    def update(s, vT_ref, c, ck):
        m_old = m_ref[...]
        m_new = jnp.maximum(m_old, jnp.max(s, axis=0, keepdims=True))
        p = jnp.exp2(s - m_new).astype(BF16)
        vT = vT_ref[0, :, pl.ds(pl.multiple_of(c * ck, ck), ck)]
        acc_ref[...] = jnp.exp2(m_old - m_new) * acc_ref[...] + _dot(vT, p)
        m_ref[...] = m_new

    segs = []
    for si in range(nseg):
        ck = min(tk, seg_lens[si])
        segs.append((kv_refs[2 * si], kv_refs[2 * si + 1], ck, seg_lens[si] // ck))
    piped = [s for s in segs if s[3] % unroll == 0]
    assert len(piped) <= 1 and unroll % 2 == 0
    if piped:
        assert piped[0][2] == s0_ref.shape[0]
        s0_ref[...] = scores(piped[0][0], 0, piped[0][2])
    for k_ref, vT_ref, ck, nch in segs:
        if nch % unroll:
            for c in range(nch):
                update(scores(k_ref, c, ck), vT_ref, c, ck)
    for k_ref, vT_ref, ck, nch in piped:
        def body(t, carry, k_ref=k_ref, vT_ref=vT_ref, ck=ck, nch=nch):
            for u in range(unroll):
                c = t * unroll + u
                s_refs[(u + 1) % 2][...] = scores(k_ref, jnp.minimum(c + 1, nch - 1), ck)
                update(s_refs[u % 2][...], vT_ref, c, ck)
            return carry

        lax.fori_loop(0, nch // unroll, body, 0)

    acc = acc_ref[...]
    o = acc[:V_DIM] * (1.0 / acc[V_DIM:V_DIM + 1])
    o_ref[...] = o.T.astype(o_ref.dtype)


def _flash(q, kvs, tq, tk):
    nh, n, _ = q.shape
    seg_lens = tuple(k.shape[1] for k, _ in kvs)
    in_specs = [pl.BlockSpec((1, tq, HEAD_PAD), lambda h, i: (h, i, 0))]
    args = [q]
    for k, vT in kvs:
        nk = k.shape[1]
        in_specs.append(pl.BlockSpec((1, nk, HEAD_PAD), lambda h, i: (h, 0, 0)))
        in_specs.append(pl.BlockSpec((1, VT_ROWS, nk), lambda h, i: (h, 0, 0)))
        args += [k, vT]
    ck = min(tk, max(seg_lens))
    return pl.pallas_call(
        functools.partial(_flash_kernel, seg_lens=seg_lens, tk=tk, unroll=FLASH_UNROLL),
        grid=(nh, n // tq),
        in_specs=in_specs,
        out_specs=pl.BlockSpec((tq, V_DIM), lambda h, i: (i, h)),
        out_shape=jax.ShapeDtypeStruct((n, nh * V_DIM), BF16),
        scratch_shapes=[pltpu.VMEM((1, tq), F32), pltpu.VMEM((VT_ROWS, tq), F32),
                        pltpu.VMEM((ck, tq), F32), pltpu.VMEM((ck, tq), F32)],
        compiler_params=_params("arbitrary", "arbitrary"),
        name="mla_flash",
    )(*args)


def _outproj_kernel(o_ref, w_ref, x_ref, mod_ref, out_ref):
    out_ref[...] = x_ref[...] + mod_ref[G1:G1 + 1] * _dot(o_ref[...], w_ref[...])


def _outproj(o, w, x, mod, tm):
    n, d = x.shape
    return pl.pallas_call(
        _outproj_kernel,
        grid=(n // tm,),
        in_specs=[
            pl.BlockSpec((tm, o.shape[1]), lambda i: (i, 0)),
            _const_spec(w.shape),
            pl.BlockSpec((tm, d), lambda i: (i, 0)),
            _const_spec(mod.shape),
        ],
        out_specs=pl.BlockSpec((tm, d), lambda i: (i, 0)),
        out_shape=jax.ShapeDtypeStruct((n, d), F32),
        compiler_params=_params("arbitrary"),
        name="mla_outproj",
    )(o, w, x, mod)


def _ffn_kernel(xp_ref, x_ref, xn_ref, mod_ref, g2_ref, wuv_ref, wug_ref, cwv_ref, cwg_ref,
                cbv_ref, cbg_ref, wd_ref, fg_ref, o_ref, h_ref, acc_ref, *, tm, final):
    i = pl.program_id(0)
    f = pl.program_id(1)
    halo = BF16_ROWS
    rows = tm + 2 * halo

    @pl.when(f == 0)
    def _():
        g, sh, sc = g2_ref[...], mod_ref[SH2:SH2 + 1], mod_ref[SC2:SC2 + 1]
        hp = jnp.where(i > 0, _norm_mod(xp_ref[...], g, sh, sc), 0.0)
        hn = jnp.where(i < pl.num_programs(0) - 1, _norm_mod(xn_ref[...], g, sh, sc), 0.0)
        h_ref[0:halo] = hp.astype(BF16)
        h_ref[halo + tm:rows] = hn.astype(BF16)

        def norm_rows(r, carry):
            r0 = pl.multiple_of(r * NORM_ROWS, NORM_ROWS)
            xr = x_ref[pl.ds(r0, NORM_ROWS), :]
            h_ref[pl.ds(halo + r0, NORM_ROWS), :] = _norm_mod(xr, g, sh, sc).astype(BF16)
            return carry

        lax.fori_loop(0, tm // NORM_ROWS, norm_rows, 0, unroll=8)
        acc_ref[...] = jnp.zeros_like(acc_ref)

    hh = h_ref[...]
    tf = wd_ref.shape[0]
    tc = min(tf, FFN_COLS)

    def conv(w_ref, cw_ref, cb_ref, cols):
        u = _dot(hh, w_ref[:, cols])
        cw = cw_ref[:, cols]
        r = (pltpu.roll(u, 1, 0) * cw[0:1] + u * cw[1:2] + pltpu.roll(u, rows - 1, 0) * cw[2:3]
             + cb_ref[:, cols])
        return r[halo:halo + tm]

    down = None
    for c0 in range(0, tf, tc):
        cols = slice(c0, c0 + tc)
        val = conv(wuv_ref, cwv_ref, cbv_ref, cols)
        gate = conv(wug_ref, cwg_ref, cbg_ref, cols)
        act = (val * (gate * (1.0 / (1.0 + jnp.exp(-gate))))).astype(BF16)
        part = _dot(act, wd_ref[cols, :])
        down = part if down is None else down + part
    acc_ref[...] += down

    @pl.when(f == pl.num_programs(1) - 1)
    def _():
        out = x_ref[...] + mod_ref[G2:G2 + 1] * acc_ref[...]
        if final:
            out = _rms(out, fg_ref[...])
        o_ref[...] = out


def _ffn(x, mod, g2, layer, w_up, conv_w, conv_b, w_down, final_g, tm, tf, final):
    n, d = x.shape
    dff = w_down.shape[1]
    nf = dff // tf
    halo = BF16_ROWS
    hb = tm // halo
    last = n // halo - 1
    return pl.pallas_call(
        functools.partial(_ffn_kernel, tm=tm, final=final),
        grid=(n // tm, nf),
        in_specs=[
            pl.BlockSpec((halo, d), lambda i, f: (jnp.maximum(i * hb - 1, 0), 0)),
            pl.BlockSpec((tm, d), lambda i, f: (i, 0)),
            pl.BlockSpec((halo, d), lambda i, f: (jnp.minimum((i + 1) * hb, last), 0)),
            _const_spec(mod.shape), _const_spec(g2.shape),
            pl.BlockSpec((None, d, tf), lambda i, f: (layer, 0, f)),
            pl.BlockSpec((None, d, tf), lambda i, f: (layer, 0, nf + f)),
            pl.BlockSpec((None, 3, tf), lambda i, f: (layer, 0, f)),
            pl.BlockSpec((None, 3, tf), lambda i, f: (layer, 0, nf + f)),
            pl.BlockSpec((None, 1, tf), lambda i, f: (layer, 0, f)),
            pl.BlockSpec((None, 1, tf), lambda i, f: (layer, 0, nf + f)),
            pl.BlockSpec((None, tf, d), lambda i, f: (layer, f, 0)),
            _const_spec(final_g.shape),
        ],
        out_specs=pl.BlockSpec((tm, d), lambda i, f: (i, 0)),
        out_shape=jax.ShapeDtypeStruct((n, d), F32),
        scratch_shapes=[pltpu.VMEM((tm + 2 * halo, d), BF16), pltpu.VMEM((tm, d), F32)],
        compiler_params=_params("arbitrary", "arbitrary"),
        name="conv_ffn",
    )(x, x, x, mod, g2, w_up, w_up, conv_w, conv_w, conv_b, conv_b, w_down, final_g)


def _channel_dft(xr, xi, cc_ref, sc_ref, dot):
    parts = []
    for g in range(FNO_GROUPS):
        lo, hi = g * FNO_GROUP_DIM, (g + 1) * FNO_GROUP_DIM
        parts.append(dot(xr[:, lo:hi], cc_ref[...]) + dot(xi[:, lo:hi], sc_ref[...]))
    return jnp.concatenate(parts, axis=1)


def _fft_stage1_kernel(x_ref, mod_ref, g1_ref, kc_ref, ks_ref, yr_ref, yi_ref):
    n1, b2, d = x_ref.shape
    x = x_ref[...].reshape(n1 * b2, d)
    h = _norm_mod(x, g1_ref[...], mod_ref[SH1:SH1 + 1], mod_ref[SC1:SC1 + 1]).astype(BF16)
    yr_ref[...] = _dot(kc_ref[...], h).reshape(n1, b2, d)
    yi_ref[...] = _dot(ks_ref[...], h).reshape(n1, b2, d)


def _fft_stage2_kernel(yr_ref, yi_ref, m_ref, cc_ref, sc_ref, p_ref, w_ref, x_ref, mod_ref, o_ref,
                       f_ref, fp_ref):
    b2 = m_ref.shape[0]
    tn = o_ref.shape[2]

    @pl.when(pl.program_id(1) == 0)
    def _():
        for q in range(b2):
            rows = slice(q * FFT_N2, (q + 1) * FFT_N2)
            ys = jnp.concatenate([yr_ref[rows], yi_ref[rows]], axis=0).astype(BF16)
            xs = _dot(m_ref[q], ys).astype(BF16)
            f_ref[rows] = _channel_dft(xs[:FFT_N2], xs[FFT_N2:], cc_ref, sc_ref, _dot).astype(BF16)
        fp_ref[...] = _dot(p_ref[...], f_ref[...]).astype(BF16)

    y = _dot(fp_ref[...], w_ref[...])
    o_ref[...] = x_ref[...] + mod_ref[G1:G1 + 1].reshape(1, 1, tn) * y.reshape(FFT_N2, b2, tn)


def _dft_small_kernel(x_ref, mod_ref, g1_ref, m_ref, cc_ref, sc_ref, w_ref, o_ref):
    n = x_ref.shape[0]
    x = x_ref[...]
    h = _norm_mod(x, g1_ref[...], mod_ref[SH1:SH1 + 1], mod_ref[SC1:SC1 + 1])
    xs = _dot_hi(m_ref[...], h)
    fmix = _channel_dft(xs[:n], xs[n:], cc_ref, sc_ref, _dot_hi).astype(BF16)
    o_ref[...] = x + mod_ref[G1:G1 + 1] * _dot(fmix, w_ref[...])


def _cos_sin(num, den, scale):
    ang = (2.0 * np.pi / den) * (num % den).astype(np.float64)
    return (np.cos(ang) * scale).astype(np.float32), (np.sin(ang) * scale).astype(np.float32)


def _dft_tables(n):
    gd = FNO_GROUP_DIM
    a = np.arange(gd, dtype=np.int64)
    cc, sc = _cos_sin(a[:, None] * a[None, :], gd, gd ** -0.5)
    if n % (FFT_N2 * FFT_B2) == 0 and n > 2 * FFT_N2:
        n1 = n // FFT_N2
        j1 = np.arange(n1, dtype=np.int64)
        c1, s1 = _cos_sin(j1[:, None] * j1[None, :], n1, n1 ** -0.5)
        eye = np.eye(FFT_B2, dtype=np.float32)
        k = j1[:, None, None] + n1 * np.arange(FFT_N2, dtype=np.int64)[None, :, None]
        j2 = np.arange(FFT_N2, dtype=np.int64)[None, None, :]
        mc, ms = _cos_sin(k * j2, n, FFT_N2 ** -0.5)
        m2 = np.concatenate([np.concatenate([mc, ms], axis=2),
                             np.concatenate([-ms, mc], axis=2)], axis=1)
        rows = np.arange(FFT_N2 * FFT_B2)
        perm = np.zeros((FFT_N2 * FFT_B2,) * 2, np.float32)
        perm[rows, (rows % FFT_B2) * FFT_N2 + rows // FFT_B2] = 1.0
        tabs = {"kc": np.kron(c1, eye), "ks": np.kron(-s1, eye), "m2": m2, "perm": perm,
                "cc": cc, "sc": sc}
        return {k_: jnp.asarray(v).astype(BF16) for k_, v in tabs.items()}
    j = np.arange(n, dtype=np.int64)
    c, s = _cos_sin(j[:, None] * j[None, :], n, n ** -0.5)
    tabs = {"m": np.concatenate([c, -s], axis=0), "cc": cc, "sc": sc}
    return {k_: jnp.asarray(v) for k_, v in tabs.items()}


def _fourier_long(x, mod, g1, layer, w, tabs, tn):
    n, d = x.shape
    n1 = n // FFT_N2
    nb = n1 * FFT_B2
    yr, yi = pl.pallas_call(
        _fft_stage1_kernel,
        grid=(FFT_N2 // FFT_B2,),
        in_specs=[
            pl.BlockSpec((n1, FFT_B2, d), lambda s: (0, s, 0)),
            _const_spec(mod.shape), _const_spec(g1.shape),
            _const_spec((nb, nb)), _const_spec((nb, nb)),
        ],
        out_specs=[pl.BlockSpec((n1, FFT_B2, d), lambda s: (0, s, 0))] * 2,
        out_shape=[jax.ShapeDtypeStruct((n1, FFT_N2, d), F32)] * 2,
        compiler_params=_params("arbitrary"),
        name="fft_stage1",
    )(x.reshape(n1, FFT_N2, d), mod, g1, tabs["kc"], tabs["ks"])
    rows = FFT_N2 * FFT_B2
    out = pl.pallas_call(
        _fft_stage2_kernel,
        grid=(n1 // FFT_B2, d // tn),
        in_specs=[
            pl.BlockSpec((rows, d), lambda a, j: (a, 0), pipeline_mode=pl.Buffered(1)),
            pl.BlockSpec((rows, d), lambda a, j: (a, 0), pipeline_mode=pl.Buffered(1)),
            pl.BlockSpec((FFT_B2, 2 * FFT_N2, 2 * FFT_N2), lambda a, j: (a, 0, 0)),
            _const_spec(tabs["cc"].shape), _const_spec(tabs["sc"].shape),
            _const_spec(tabs["perm"].shape),
            pl.BlockSpec((None, d, tn), lambda a, j: (layer, 0, j)),
            pl.BlockSpec((FFT_N2, FFT_B2, tn), lambda a, j: (0, a, j)),
            pl.BlockSpec((N_MOD, tn), lambda a, j: (0, j)),
        ],
        out_specs=pl.BlockSpec((FFT_N2, FFT_B2, tn), lambda a, j: (0, a, j)),
        out_shape=jax.ShapeDtypeStruct((FFT_N2, n1, d), F32),
        scratch_shapes=[pltpu.VMEM((rows, d), BF16), pltpu.VMEM((rows, d), BF16)],
        compiler_params=_params("arbitrary", "arbitrary"),
        name="fft_stage2",
    )(yr.reshape(n, d), yi.reshape(n, d), tabs["m2"], tabs["cc"], tabs["sc"], tabs["perm"], w,
      x.reshape(FFT_N2, n1, d), mod)
    return out.reshape(n, d)


def _fourier_short(x, mod, g1, layer, w, tabs):
    n, d = x.shape
    return pl.pallas_call(
        _dft_small_kernel,
        grid=(1,),
        in_specs=[_const_spec(a.shape) for a in (x, mod, g1, tabs["m"], tabs["cc"], tabs["sc"])]
        + [pl.BlockSpec((None, d, d), lambda i: (layer, 0, 0))],
        out_specs=pl.BlockSpec((n, d), lambda i: (0, 0)),
        out_shape=jax.ShapeDtypeStruct((n, d), F32),
        compiler_params=_params("arbitrary"),
        name="dft_ctx",
    )(x, mod, g1, tabs["m"], tabs["cc"], tabs["sc"], w)


def _fourier(x, mod, g1, layer, w):
    tabs = _dft_tables(x.shape[0])
    if "m2" in tabs:
        return _fourier_long(x, mod, g1, layer, w, tabs, 512)
    return _fourier_short(x, mod, g1, layer, w, tabs)


def _rot_cols(w):
    q = QK_ROPE // 4
    a1, a2, b1, b2 = w[..., :q], w[..., q:2 * q], w[..., 2 * q:3 * q], w[..., 3 * q:]
    return jnp.concatenate([-a2, a1, -b2, b1], axis=-1)


def _pad_lanes(w):
    return jnp.pad(w, [(0, 0)] * (w.ndim - 1) + [(0, LANES - w.shape[-1])])


def _mla_weights(w_dqkv, g_q, g_kv, w_uq, w_ukv):
    hd = N_HEADS * LANES
    kr = w_dqkv[:, Q_LORA + KV_LORA:]
    wd = jnp.concatenate([w_dqkv[:, :Q_LORA + KV_LORA], _pad_lanes(kr), _pad_lanes(_rot_cols(kr))],
                         axis=1).astype(BF16)
    q3 = w_uq.reshape(Q_LORA, N_HEADS, QK_NOPE + QK_ROPE)
    qr = q3[:, :, QK_NOPE:]
    wq = jnp.concatenate([q3[:, :, :QK_NOPE].reshape(Q_LORA, hd),
                          _pad_lanes(qr).reshape(Q_LORA, hd),
                          _pad_lanes(_rot_cols(qr)).reshape(Q_LORA, hd)], axis=1).astype(BF16)
    kv3 = w_ukv.reshape(KV_LORA, N_HEADS, QK_NOPE + V_DIM)
    wk = kv3[:, :, :QK_NOPE].reshape(KV_LORA, hd).astype(BF16)
    wvT = kv3[:, :, QK_NOPE:].reshape(KV_LORA, N_HEADS * V_DIM).T.astype(BF16)
    return wd, g_q.reshape(1, -1), g_kv.reshape(1, -1), wq, wk, wvT


def _rope_tables(n):
    rows = n // GRID_W
    r, col = jnp.meshgrid(jnp.arange(rows, dtype=F32), jnp.arange(GRID_W, dtype=F32), indexing="ij")
    half = QK_ROPE // 2
    inv_freq = jnp.power(ROPE_BASE, -jnp.arange(0, half, 2, dtype=F32) / half)
    ang_r = r.reshape(-1)[:, None] * inv_freq
    ang_c = col.reshape(-1)[:, None] * inv_freq
    ang = jnp.concatenate([ang_r, ang_r, ang_c, ang_c], axis=-1)
    return _pad_lanes(jnp.cos(ang)), _pad_lanes(jnp.sin(ang))


def _row_tile(n, pref):
    return pref if n % pref == 0 else n


def kernel(x, c, ctx, c_ctx, ada_w, ada_b, norm1_g, norm2_g, mla_w_dqkv, mla_q_norm_g, mla_kv_norm_g,
           mla_w_uq, mla_w_ukv, mla_w_o, fno_w, ffn_w_up, ffn_conv_w, ffn_conv_b, ffn_w_down,
           final_norm_g):
    assert x.shape[0] == 1 and c.shape[0] == 1 and ctx.shape[0] == 1
    xs = x[0]
    cs = ctx[0]
    n, nc = xs.shape[0], cs.shape[0]
    mods = _ada_all(jnp.concatenate([c, c_ctx[None, :]], axis=0), ada_w, ada_b)
    cos, sin = _rope_tables(n)
    ones_c = _pad_lanes(jnp.ones((nc, QK_ROPE), F32))
    zeros_c = jnp.zeros((nc, LANES), F32)
    fin_g = final_norm_g.reshape(1, -1)
    w_up = ffn_w_up.astype(BF16)
    w_down = ffn_w_down.astype(BF16)
    w_fno = fno_w.astype(BF16)
    conv_b = ffn_conv_b.reshape(DEPTH, 1, -1)

    for i in range(DEPTH):
        kind = i % N_MIXERS
        j = i // N_MIXERS
        ctx_later = any(l % N_MIXERS == MIXER_MLA for l in range(i + 1, DEPTH))
        mod_x, mod_c = mods[i, 0], mods[i, 1]
        g1 = norm1_g[i].reshape(1, -1)
        g2 = norm2_g[i].reshape(1, -1)
        if kind == MIXER_MLA:
            wts = _mla_weights(mla_w_dqkv[j], mla_q_norm_g[j], mla_kv_norm_g[j], mla_w_uq[j], mla_w_ukv[j])
            w_o = mla_w_o[j].astype(BF16)
            qx, kx, vTx = _mla_proj(xs, mod_x, g1, wts, cos, sin, _row_tile(n, 256))
            qc, kc, vTc = _mla_proj(cs, mod_c, g1, wts, ones_c, zeros_c, _row_tile(nc, 256))
            ox = _flash(qx, [(kc, vTc), (kx, vTx)], _row_tile(n, 1024), 512)
            xs = _outproj(ox, w_o, xs, mod_x, _row_tile(n, 512))
            if ctx_later:
                oc = _flash(qc, [(kc, vTc)], _row_tile(nc, 256), 512)
                cs = _outproj(oc, w_o, cs, mod_c, _row_tile(nc, 256))
        else:
            xs = _fourier(xs, mod_x, g1, j, w_fno)
            if ctx_later:
                cs = _fourier(cs, mod_c, g1, j, w_fno)
        xs = _ffn(xs, mod_x, g2, i, w_up, ffn_conv_w, conv_b, w_down, fin_g, _row_tile(n, 512), 512,
                  final=(i == DEPTH - 1))
        if ctx_later:
            cs = _ffn(cs, mod_c, g2, i, w_up, ffn_conv_w, conv_b, w_down, fin_g, _row_tile(nc, 256), 512,
                      final=False)
    return xs[None]
```

```python
import functools
import math

import jax
import jax.numpy as jnp
import numpy as np
from jax import lax
from jax.experimental import pallas as pl
from jax.experimental.pallas import tpu as pltpu

D_MODEL = 2048
DEPTH = 4
GRID_W = 64
N_MIXERS = 2
MIXER_MLA = 0
N_HEADS = 16
Q_LORA = 512
KV_LORA = 512
QK_NOPE = 128
QK_ROPE = 64
V_DIM = 128
ROPE_BASE = 10000.0
SM_SCALE = (QK_NOPE + QK_ROPE) ** -0.5
FNO_GROUPS = 8
FNO_GROUP_DIM = D_MODEL // FNO_GROUPS
D_FF = 5632
NORM_EPS = 1e-6
N_MOD = 6

LANES = 128
SUBLANES = 8
BF16_ROWS = 16
FLASH_RING = 4
FLASH_AHEAD = 2
NORM_ROWS = BF16_ROWS
FFN_COLS = 256
VMEM_LIMIT_BYTES = 56 * 1024 * 1024

HEAD_PAD = 2 * LANES
VT_ROWS = V_DIM + SUBLANES
Q_SCALE = SM_SCALE * math.log2(math.e)
FFT_N2 = 128
FFT_B2 = SUBLANES

BF16 = jnp.bfloat16
F32 = jnp.float32
HIGHEST = lax.Precision.HIGHEST

SH1, SC1, G1, SH2, SC2, G2 = range(6)


def _params(*sem):
    return pltpu.CompilerParams(dimension_semantics=sem, vmem_limit_bytes=VMEM_LIMIT_BYTES)


def _const_spec(shape):
    nd = len(shape)
    return pl.BlockSpec(shape, lambda *_: (0,) * nd, pipeline_mode=pl.Buffered(1))


def _dot(a, b):
    return jnp.dot(a, b, preferred_element_type=F32)


def _dot_hi(a, b):
    return jnp.dot(a, b, preferred_element_type=F32, precision=HIGHEST)


def _rms(x, g):
    return x * lax.rsqrt(jnp.mean(x * x, axis=-1, keepdims=True) + NORM_EPS) * g


def _norm_mod(x, g, shift, scale):
    return _rms(x, g) * (1.0 + scale) + shift


def _ada_kernel(condT_ref, w_ref, b_ref, o_ref, acc_ref):
    k = pl.program_id(2)
    tk, tn = w_ref.shape[1], w_ref.shape[2]
    n_rows = acc_ref.shape[0]

    @pl.when(k == 0)
    def _():
        acc_ref[...] = jnp.zeros_like(acc_ref)

    w = w_ref[0]
    ct = condT_ref[pl.ds(pl.multiple_of(k * tk, tk), tk), :]
    s = ct * (1.0 / (1.0 + jnp.exp(-ct)))
    for r in range(n_rows):
        prod = w * s[:, r:r + 1]
        acc_ref[r] += prod.reshape(tk // SUBLANES, SUBLANES, tn).sum(axis=0)

    @pl.when(k == pl.num_programs(2) - 1)
    def _():
        for r in range(n_rows):
            o_ref[0, r:r + 1, :] = acc_ref[r].sum(axis=0, keepdims=True) + b_ref[0]


def _ada_all(cond, ada_w, ada_b):
    n_rows, d = cond.shape
    depth, _, nout = ada_w.shape
    tk, tn = 1024, 2048
    out = pl.pallas_call(
        _ada_kernel,
        grid=(depth, nout // tn, d // tk),
        in_specs=[
            pl.BlockSpec((d, n_rows), lambda l, j, k: (0, 0)),
            pl.BlockSpec((1, tk, tn), lambda l, j, k: (l, k, j)),
            pl.BlockSpec((1, 1, tn), lambda l, j, k: (l, 0, j)),
        ],
        out_specs=pl.BlockSpec((1, n_rows, tn), lambda l, j, k: (l, 0, j)),
        out_shape=jax.ShapeDtypeStruct((depth, n_rows, nout), F32),
        scratch_shapes=[pltpu.VMEM((n_rows, SUBLANES, tn), F32)],
        compiler_params=_params("arbitrary", "arbitrary", "arbitrary"),
        name="ada_mod",
    )(cond.T, ada_w, ada_b.reshape(depth, 1, nout))
    return out.reshape(depth, n_rows, N_MOD, d)


def _mla_proj_kernel(x_ref, mod_ref, g1_ref, wd_ref, gq_ref, gkv_ref, wq_ref, wk_ref, wvT_ref,
                     cos_ref, sin_ref, q_ref, k_ref, vT_ref):
    hd = N_HEADS * LANES
    h = _norm_mod(x_ref[...], g1_ref[...], mod_ref[SH1:SH1 + 1], mod_ref[SC1:SC1 + 1])
    d = _dot(h.astype(BF16), wd_ref[...])
    cq = _rms(d[:, :Q_LORA], gq_ref[...]).astype(BF16)
    ckv = _rms(d[:, Q_LORA:Q_LORA + KV_LORA], gkv_ref[...]).astype(BF16)
    cos = cos_ref[...]
    sin = sin_ref[...]
    base = Q_LORA + KV_LORA
    kr = (d[:, base:base + LANES] * cos + d[:, base + LANES:base + 2 * LANES] * sin).astype(BF16)
    qall = _dot(cq, wq_ref[...])
    kn = _dot(ckv, wk_ref[...])
    vT = lax.dot_general(wvT_ref[...], ckv, (((1,), (1,)), ((), ())),
                         preferred_element_type=F32)
    tm = x_ref.shape[0]
    ones_row = (lax.broadcasted_iota(jnp.int32, (VT_ROWS - V_DIM, tm), 0) == 0).astype(BF16)
    for hh in range(N_HEADS):
        lo, hi = hh * LANES, (hh + 1) * LANES
        qr = qall[:, hd + lo:hd + hi] * cos + qall[:, 2 * hd + lo:2 * hd + hi] * sin
        q_ref[hh, :, 0:LANES] = (qall[:, lo:hi] * Q_SCALE).astype(BF16)
        q_ref[hh, :, LANES:HEAD_PAD] = (qr * Q_SCALE).astype(BF16)
        k_ref[hh, :, 0:LANES] = kn[:, lo:hi].astype(BF16)
        k_ref[hh, :, LANES:HEAD_PAD] = kr
        vT_ref[hh, 0:V_DIM, :] = vT[lo:hi, :].astype(BF16)
        vT_ref[hh, V_DIM:VT_ROWS, :] = ones_row


def _mla_proj(x, mod, g1, wts, cos, sin, tm):
    n, d = x.shape
    wd, gq, gkv, wq, wk, wvT = wts
    return pl.pallas_call(
        _mla_proj_kernel,
        grid=(n // tm,),
        in_specs=[
            pl.BlockSpec((tm, d), lambda i: (i, 0)),
            _const_spec(mod.shape), _const_spec(g1.shape), _const_spec(wd.shape),
            _const_spec(gq.shape), _const_spec(gkv.shape), _const_spec(wq.shape),
            _const_spec(wk.shape), _const_spec(wvT.shape),
            pl.BlockSpec((tm, LANES), lambda i: (i, 0)),
            pl.BlockSpec((tm, LANES), lambda i: (i, 0)),
        ],
        out_specs=[
            pl.BlockSpec((N_HEADS, tm, HEAD_PAD), lambda i: (0, i, 0)),
            pl.BlockSpec((N_HEADS, tm, HEAD_PAD), lambda i: (0, i, 0)),
            pl.BlockSpec((N_HEADS, VT_ROWS, tm), lambda i: (0, 0, i)),
        ],
        out_shape=[
            jax.ShapeDtypeStruct((N_HEADS, n, HEAD_PAD), BF16),
            jax.ShapeDtypeStruct((N_HEADS, n, HEAD_PAD), BF16),
            jax.ShapeDtypeStruct((N_HEADS, VT_ROWS, n), BF16),
        ],
        compiler_params=_params("arbitrary"),
        name="mla_proj",
    )(x, mod, g1, wd, gq, gkv, wq, wk, wvT, cos, sin)


def _flash_kernel(*refs, seg_lens, tk):
    nseg = len(seg_lens)
    q_ref = refs[0]
    kv_refs = refs[1:1 + 2 * nseg]
    o_ref = refs[1 + 2 * nseg]
    m_ref, acc_ref, s_ref = refs[2 + 2 * nseg:]
    ring = s_ref.shape[0]
    q = q_ref[0]
    m_ref[...] = jnp.full_like(m_ref, -jnp.inf)
    acc_ref[...] = jnp.zeros_like(acc_ref)

    def chunk(c, ck):
        return pl.ds(c * ck if isinstance(c, int) else pl.multiple_of(c * ck, ck), ck)

    def scores(k_ref, c, ck):
        k = k_ref[0, chunk(c, ck), :]
        return lax.dot_general(k, q, (((1,), (1,)), ((), ())), preferred_element_type=F32)

    def update(s, vT_ref, c, ck):
        m_old = m_ref[...]
        m_new = jnp.maximum(m_old, jnp.max(s, axis=0, keepdims=True))
        p = jnp.exp2(s - m_new).astype(BF16)
        vT = vT_ref[0, :, chunk(c, ck)]
        acc_ref[...] = jnp.exp2(m_old - m_new) * acc_ref[...] + _dot(vT, p)
        m_ref[...] = m_new

    segs = []
    for si in range(nseg):
        ck = min(tk, seg_lens[si])
        segs.append((kv_refs[2 * si], kv_refs[2 * si + 1], ck, seg_lens[si] // ck))
    is_piped = lambda s: s[3] % ring == 0 and s[3] >= 2 * ring
    piped = [s for s in segs if is_piped(s)]
    assert len(piped) <= 1 and FLASH_AHEAD < ring
    if piped:
        assert piped[0][2] == s_ref.shape[1]
        for a in range(FLASH_AHEAD):
            s_ref[a] = scores(piped[0][0], a, piped[0][2])
    for seg in segs:
        if not is_piped(seg):
            k_ref, vT_ref, ck, nch = seg
            for c in range(nch):
                update(scores(k_ref, c, ck), vT_ref, c, ck)
    for k_ref, vT_ref, ck, nch in piped:
        def step(c, u, prefetch):
            if prefetch:
                s_ref[(u + FLASH_AHEAD) % ring] = scores(k_ref, c + FLASH_AHEAD, ck)
            update(s_ref[u], vT_ref, c, ck)

        def body(t, carry):
            for u in range(ring):
                step(t * ring + u, u, True)
            return carry

        lax.fori_loop(0, nch // ring - 1, body, 0)
        for u in range(ring):
            c = nch - ring + u
            step(c, u, c + FLASH_AHEAD < nch)

    acc = acc_ref[...]
    o = acc[:V_DIM] * (1.0 / acc[V_DIM:V_DIM + 1])
    o_ref[...] = o.T.astype(o_ref.dtype)


def _flash(q, kvs, tq, tk):
    nh, n, _ = q.shape
    seg_lens = tuple(k.shape[1] for k, _ in kvs)
    in_specs = [pl.BlockSpec((1, tq, HEAD_PAD), lambda h, i: (h, i, 0))]
    args = [q]
    for k, vT in kvs:
        nk = k.shape[1]
        in_specs.append(pl.BlockSpec((1, nk, HEAD_PAD), lambda h, i: (h, 0, 0)))
        in_specs.append(pl.BlockSpec((1, VT_ROWS, nk), lambda h, i: (h, 0, 0)))
        args += [k, vT]
    ck = min(tk, max(seg_lens))
    return pl.pallas_call(
        functools.partial(_flash_kernel, seg_lens=seg_lens, tk=tk),
        grid=(nh, n // tq),
        in_specs=in_specs,
        out_specs=pl.BlockSpec((tq, V_DIM), lambda h, i: (i, h)),
        out_shape=jax.ShapeDtypeStruct((n, nh * V_DIM), BF16),
        scratch_shapes=[pltpu.VMEM((1, tq), F32), pltpu.VMEM((VT_ROWS, tq), F32),
                        pltpu.VMEM((FLASH_RING, ck, tq), F32)],
        compiler_params=_params("arbitrary", "arbitrary"),
        name="mla_flash",
    )(*args)


def _outproj_kernel(o_ref, w_ref, x_ref, mod_ref, out_ref):
    out_ref[...] = x_ref[...] + mod_ref[G1:G1 + 1] * _dot(o_ref[...], w_ref[...])


def _outproj(o, w, x, mod, tm):
    n, d = x.shape
    return pl.pallas_call(
        _outproj_kernel,
        grid=(n // tm,),
        in_specs=[
            pl.BlockSpec((tm, o.shape[1]), lambda i: (i, 0)),
            _const_spec(w.shape),
            pl.BlockSpec((tm, d), lambda i: (i, 0)),
            _const_spec(mod.shape),
        ],
        out_specs=pl.BlockSpec((tm, d), lambda i: (i, 0)),
        out_shape=jax.ShapeDtypeStruct((n, d), F32),
        compiler_params=_params("arbitrary"),
        name="mla_outproj",
    )(o, w, x, mod)


def _ffn_kernel(xp_ref, x_ref, xn_ref, mod_ref, g2_ref, wuv_ref, wug_ref, cwv_ref, cwg_ref,
                cbv_ref, cbg_ref, wd_ref, fg_ref, o_ref, h_ref, acc_ref, u_ref, *, tm, final):
    i = pl.program_id(0)
    f = pl.program_id(1)
    halo = BF16_ROWS
    rows = tm + 2 * halo

    @pl.when(f == 0)
    def _():
        g, sh, sc = g2_ref[...], mod_ref[SH2:SH2 + 1], mod_ref[SC2:SC2 + 1]
        hp = jnp.where(i > 0, _norm_mod(xp_ref[...], g, sh, sc), 0.0)
        hn = jnp.where(i < pl.num_programs(0) - 1, _norm_mod(xn_ref[...], g, sh, sc), 0.0)
        h_ref[0:halo] = hp.astype(BF16)
        h_ref[halo + tm:rows] = hn.astype(BF16)

        def norm_rows(r, carry):
            r0 = pl.multiple_of(r * NORM_ROWS, NORM_ROWS)
            xr = x_ref[pl.ds(r0, NORM_ROWS), :]
            h_ref[pl.ds(halo + r0, NORM_ROWS), :] = _norm_mod(xr, g, sh, sc).astype(BF16)
            return carry

        lax.fori_loop(0, tm // NORM_ROWS, norm_rows, 0, unroll=8)
        acc_ref[...] = jnp.zeros_like(acc_ref)

    hh = h_ref[...]
    tf = wd_ref.shape[0]
    tc = min(tf, FFN_COLS)

    def conv(slot, cw_ref, cb_ref, cols):
        cw = cw_ref[:, cols]
        return (u_ref[slot, halo - 1:halo - 1 + tm, :] * cw[0:1]
                + u_ref[slot, halo:halo + tm, :] * cw[1:2]
                + u_ref[slot, halo + 1:halo + 1 + tm, :] * cw[2:3] + cb_ref[:, cols])

    groups = [slice(c0, c0 + tc) for c0 in range(0, tf, tc)]
    for gi, cols in enumerate(groups):
        u_ref[2 * gi] = _dot(hh, wuv_ref[:, cols])
        u_ref[2 * gi + 1] = _dot(hh, wug_ref[:, cols])
    for gi, cols in enumerate(groups):
        val = conv(2 * gi, cwv_ref, cbv_ref, cols)
        gate = conv(2 * gi + 1, cwg_ref, cbg_ref, cols)
        act = (val * (gate * (1.0 / (1.0 + jnp.exp(-gate))))).astype(BF16)
        acc_ref[...] += _dot(act, wd_ref[cols, :])

    @pl.when(f == pl.num_programs(1) - 1)
    def _():
        out = x_ref[...] + mod_ref[G2:G2 + 1] * acc_ref[...]
        if final:
            out = _rms(out, fg_ref[...])
        o_ref[...] = out


def _ffn(x, mod, g2, layer, w_up, conv_w, conv_b, w_down, final_g, tm, tf, final):
    n, d = x.shape
    dff = w_down.shape[1]
    nf = dff // tf
    halo = BF16_ROWS
    hb = tm // halo
    last = n // halo - 1
    return pl.pallas_call(
        functools.partial(_ffn_kernel, tm=tm, final=final),
        grid=(n // tm, nf),
        in_specs=[
            pl.BlockSpec((halo, d), lambda i, f: (jnp.maximum(i * hb - 1, 0), 0)),
            pl.BlockSpec((tm, d), lambda i, f: (i, 0)),
            pl.BlockSpec((halo, d), lambda i, f: (jnp.minimum((i + 1) * hb, last), 0)),
            _const_spec(mod.shape), _const_spec(g2.shape),
            pl.BlockSpec((None, d, tf), lambda i, f: (layer, 0, f)),
            pl.BlockSpec((None, d, tf), lambda i, f: (layer, 0, nf + f)),
            pl.BlockSpec((None, 3, tf), lambda i, f: (layer, 0, f)),
            pl.BlockSpec((None, 3, tf), lambda i, f: (layer, 0, nf + f)),
            pl.BlockSpec((None, 1, tf), lambda i, f: (layer, 0, f)),
            pl.BlockSpec((None, 1, tf), lambda i, f: (layer, 0, nf + f)),
            pl.BlockSpec((None, tf, d), lambda i, f: (layer, f, 0)),
            _const_spec(final_g.shape),
        ],
        out_specs=pl.BlockSpec((tm, d), lambda i, f: (i, 0)),
        out_shape=jax.ShapeDtypeStruct((n, d), F32),
        scratch_shapes=[pltpu.VMEM((tm + 2 * halo, d), BF16), pltpu.VMEM((tm, d), F32),
                        pltpu.VMEM((2 * (tf // min(tf, FFN_COLS)), tm + 2 * halo, min(tf, FFN_COLS)), F32)],
        compiler_params=_params("arbitrary", "arbitrary"),
        name="conv_ffn",
    )(x, x, x, mod, g2, w_up, w_up, conv_w, conv_w, conv_b, conv_b, w_down, final_g)


def _channel_dft(xr, xi, cc_ref, sc_ref, dot):
    parts = []
    for g in range(FNO_GROUPS):
        lo, hi = g * FNO_GROUP_DIM, (g + 1) * FNO_GROUP_DIM
        parts.append(dot(xr[:, lo:hi], cc_ref[...]) + dot(xi[:, lo:hi], sc_ref[...]))
    return jnp.concatenate(parts, axis=1)


def _fft_stage1_kernel(x_ref, mod_ref, g1_ref, kc_ref, ks_ref, yr_ref, yi_ref):
    n1, b2, d = x_ref.shape
    x = x_ref[...].reshape(n1 * b2, d)
    h = _norm_mod(x, g1_ref[...], mod_ref[SH1:SH1 + 1], mod_ref[SC1:SC1 + 1]).astype(BF16)
    yr_ref[...] = _dot(kc_ref[...], h).reshape(n1, b2, d)
    yi_ref[...] = _dot(ks_ref[...], h).reshape(n1, b2, d)


def _fft_stage2_kernel(yr_ref, yi_ref, m_ref, cc_ref, sc_ref, p_ref, w_ref, x_ref, mod_ref, o_ref,
                       f_ref, fp_ref):
    b2 = m_ref.shape[0]
    tn = o_ref.shape[2]

    @pl.when(pl.program_id(1) == 0)
    def _():
        for q in range(b2):
            rows = slice(q * FFT_N2, (q + 1) * FFT_N2)
            ys = jnp.concatenate([yr_ref[rows], yi_ref[rows]], axis=0).astype(BF16)
            xs = _dot(m_ref[q], ys).astype(BF16)
            f_ref[rows] = _channel_dft(xs[:FFT_N2], xs[FFT_N2:], cc_ref, sc_ref, _dot).astype(BF16)
        fp_ref[...] = _dot(p_ref[...], f_ref[...]).astype(BF16)

    y = _dot(fp_ref[...], w_ref[...])
    o_ref[...] = x_ref[...] + mod_ref[G1:G1 + 1].reshape(1, 1, tn) * y.reshape(FFT_N2, b2, tn)


def _dft_small_kernel(x_ref, mod_ref, g1_ref, m_ref, cc_ref, sc_ref, w_ref, o_ref):
    n = x_ref.shape[0]
    x = x_ref[...]
    h = _norm_mod(x, g1_ref[...], mod_ref[SH1:SH1 + 1], mod_ref[SC1:SC1 + 1])
    xs = _dot_hi(m_ref[...], h)
    fmix = _channel_dft(xs[:n], xs[n:], cc_ref, sc_ref, _dot_hi).astype(BF16)
    o_ref[...] = x + mod_ref[G1:G1 + 1] * _dot(fmix, w_ref[...])


def _cos_sin(num, den, scale):
    ang = (2.0 * np.pi / den) * (num % den).astype(np.float64)
    return (np.cos(ang) * scale).astype(np.float32), (np.sin(ang) * scale).astype(np.float32)


def _dft_tables(n):
    gd = FNO_GROUP_DIM
    a = np.arange(gd, dtype=np.int64)
    cc, sc = _cos_sin(a[:, None] * a[None, :], gd, gd ** -0.5)
    if n % (FFT_N2 * FFT_B2) == 0 and n > 2 * FFT_N2:
        n1 = n // FFT_N2
        j1 = np.arange(n1, dtype=np.int64)
        c1, s1 = _cos_sin(j1[:, None] * j1[None, :], n1, n1 ** -0.5)
        eye = np.eye(FFT_B2, dtype=np.float32)
        k = j1[:, None, None] + n1 * np.arange(FFT_N2, dtype=np.int64)[None, :, None]
        j2 = np.arange(FFT_N2, dtype=np.int64)[None, None, :]
        mc, ms = _cos_sin(k * j2, n, FFT_N2 ** -0.5)
        m2 = np.concatenate([np.concatenate([mc, ms], axis=2),
                             np.concatenate([-ms, mc], axis=2)], axis=1)
        rows = np.arange(FFT_N2 * FFT_B2)
        perm = np.zeros((FFT_N2 * FFT_B2,) * 2, np.float32)
        perm[rows, (rows % FFT_B2) * FFT_N2 + rows // FFT_B2] = 1.0
        tabs = {"kc": np.kron(c1, eye), "ks": np.kron(-s1, eye), "m2": m2, "perm": perm,
                "cc": cc, "sc": sc}
        return {k_: jnp.asarray(v).astype(BF16) for k_, v in tabs.items()}
    j = np.arange(n, dtype=np.int64)
    c, s = _cos_sin(j[:, None] * j[None, :], n, n ** -0.5)
    tabs = {"m": np.concatenate([c, -s], axis=0), "cc": cc, "sc": sc}
    return {k_: jnp.asarray(v) for k_, v in tabs.items()}


def _fourier_long(x, mod, g1, layer, w, tabs, tn):
    n, d = x.shape
    n1 = n // FFT_N2
    nb = n1 * FFT_B2
    yr, yi = pl.pallas_call(
        _fft_stage1_kernel,
        grid=(FFT_N2 // FFT_B2,),
        in_specs=[
            pl.BlockSpec((n1, FFT_B2, d), lambda s: (0, s, 0)),
            _const_spec(mod.shape), _const_spec(g1.shape),
            _const_spec((nb, nb)), _const_spec((nb, nb)),
        ],
        out_specs=[pl.BlockSpec((n1, FFT_B2, d), lambda s: (0, s, 0))] * 2,
        out_shape=[jax.ShapeDtypeStruct((n1, FFT_N2, d), F32)] * 2,
        compiler_params=_params("arbitrary"),
        name="fft_stage1",
    )(x.reshape(n1, FFT_N2, d), mod, g1, tabs["kc"], tabs["ks"])
    rows = FFT_N2 * FFT_B2
    out = pl.pallas_call(
        _fft_stage2_kernel,
        grid=(n1 // FFT_B2, d // tn),
        in_specs=[
            pl.BlockSpec((rows, d), lambda a, j: (a, 0), pipeline_mode=pl.Buffered(1)),
            pl.BlockSpec((rows, d), lambda a, j: (a, 0), pipeline_mode=pl.Buffered(1)),
            pl.BlockSpec((FFT_B2, 2 * FFT_N2, 2 * FFT_N2), lambda a, j: (a, 0, 0)),
            _const_spec(tabs["cc"].shape), _const_spec(tabs["sc"].shape),
            _const_spec(tabs["perm"].shape),
            pl.BlockSpec((None, d, tn), lambda a, j: (layer, 0, j)),
            pl.BlockSpec((FFT_N2, FFT_B2, tn), lambda a, j: (0, a, j)),
            pl.BlockSpec((N_MOD, tn), lambda a, j: (0, j)),
        ],
        out_specs=pl.BlockSpec((FFT_N2, FFT_B2, tn), lambda a, j: (0, a, j)),
        out_shape=jax.ShapeDtypeStruct((FFT_N2, n1, d), F32),
        scratch_shapes=[pltpu.VMEM((rows, d), BF16), pltpu.VMEM((rows, d), BF16)],
        compiler_params=_params("arbitrary", "arbitrary"),
        name="fft_stage2",
    )(yr.reshape(n, d), yi.reshape(n, d), tabs["m2"], tabs["cc"], tabs["sc"], tabs["perm"], w,
      x.reshape(FFT_N2, n1, d), mod)
    return out.reshape(n, d)


def _fourier_short(x, mod, g1, layer, w, tabs):
    n, d = x.shape
    return pl.pallas_call(
        _dft_small_kernel,
        grid=(1,),
        in_specs=[_const_spec(a.shape) for a in (x, mod, g1, tabs["m"], tabs["cc"], tabs["sc"])]
        + [pl.BlockSpec((None, d, d), lambda i: (layer, 0, 0))],
        out_specs=pl.BlockSpec((n, d), lambda i: (0, 0)),
        out_shape=jax.ShapeDtypeStruct((n, d), F32),
        compiler_params=_params("arbitrary"),
        name="dft_ctx",
    )(x, mod, g1, tabs["m"], tabs["cc"], tabs["sc"], w)


def _fourier(x, mod, g1, layer, w):
    tabs = _dft_tables(x.shape[0])
    if "m2" in tabs:
        return _fourier_long(x, mod, g1, layer, w, tabs, 512)
    return _fourier_short(x, mod, g1, layer, w, tabs)


def _rot_cols(w):
    q = QK_ROPE // 4
    a1, a2, b1, b2 = w[..., :q], w[..., q:2 * q], w[..., 2 * q:3 * q], w[..., 3 * q:]
    return jnp.concatenate([-a2, a1, -b2, b1], axis=-1)


def _pad_lanes(w):
    return jnp.pad(w, [(0, 0)] * (w.ndim - 1) + [(0, LANES - w.shape[-1])])


def _mla_weights(w_dqkv, g_q, g_kv, w_uq, w_ukv):
    hd = N_HEADS * LANES
    kr = w_dqkv[:, Q_LORA + KV_LORA:]
    wd = jnp.concatenate([w_dqkv[:, :Q_LORA + KV_LORA], _pad_lanes(kr), _pad_lanes(_rot_cols(kr))],
                         axis=1).astype(BF16)
    q3 = w_uq.reshape(Q_LORA, N_HEADS, QK_NOPE + QK_ROPE)
    qr = q3[:, :, QK_NOPE:]
    wq = jnp.concatenate([q3[:, :, :QK_NOPE].reshape(Q_LORA, hd),
                          _pad_lanes(qr).reshape(Q_LORA, hd),
                          _pad_lanes(_rot_cols(qr)).reshape(Q_LORA, hd)], axis=1).astype(BF16)
    kv3 = w_ukv.reshape(KV_LORA, N_HEADS, QK_NOPE + V_DIM)
    wk = kv3[:, :, :QK_NOPE].reshape(KV_LORA, hd).astype(BF16)
    wvT = kv3[:, :, QK_NOPE:].reshape(KV_LORA, N_HEADS * V_DIM).T.astype(BF16)
    return wd, g_q.reshape(1, -1), g_kv.reshape(1, -1), wq, wk, wvT


def _rope_tables(n):
    rows = n // GRID_W
    r, col = jnp.meshgrid(jnp.arange(rows, dtype=F32), jnp.arange(GRID_W, dtype=F32), indexing="ij")
    half = QK_ROPE // 2
    inv_freq = jnp.power(ROPE_BASE, -jnp.arange(0, half, 2, dtype=F32) / half)
    ang_r = r.reshape(-1)[:, None] * inv_freq
    ang_c = col.reshape(-1)[:, None] * inv_freq
    ang = jnp.concatenate([ang_r, ang_r, ang_c, ang_c], axis=-1)
    return _pad_lanes(jnp.cos(ang)), _pad_lanes(jnp.sin(ang))


def _row_tile(n, pref):
    return pref if n % pref == 0 else n


def kernel(x, c, ctx, c_ctx, ada_w, ada_b, norm1_g, norm2_g, mla_w_dqkv, mla_q_norm_g, mla_kv_norm_g,
           mla_w_uq, mla_w_ukv, mla_w_o, fno_w, ffn_w_up, ffn_conv_w, ffn_conv_b, ffn_w_down,
           final_norm_g):
    assert x.shape[0] == 1 and c.shape[0] == 1 and ctx.shape[0] == 1
    xs = x[0]
    cs = ctx[0]
    n, nc = xs.shape[0], cs.shape[0]
    mods = _ada_all(jnp.concatenate([c, c_ctx[None, :]], axis=0), ada_w, ada_b)
    cos, sin = _rope_tables(n)
    ones_c = _pad_lanes(jnp.ones((nc, QK_ROPE), F32))
    zeros_c = jnp.zeros((nc, LANES), F32)
    fin_g = final_norm_g.reshape(1, -1)
    w_up = ffn_w_up.astype(BF16)
    w_down = ffn_w_down.astype(BF16)
    w_fno = fno_w.astype(BF16)
    conv_b = ffn_conv_b.reshape(DEPTH, 1, -1)

    for i in range(DEPTH):
        kind = i % N_MIXERS
        j = i // N_MIXERS
        ctx_later = any(l % N_MIXERS == MIXER_MLA for l in range(i + 1, DEPTH))
        mod_x, mod_c = mods[i, 0], mods[i, 1]
        g1 = norm1_g[i].reshape(1, -1)
        g2 = norm2_g[i].reshape(1, -1)
        if kind == MIXER_MLA:
            wts = _mla_weights(mla_w_dqkv[j], mla_q_norm_g[j], mla_kv_norm_g[j], mla_w_uq[j], mla_w_ukv[j])
            w_o = mla_w_o[j].astype(BF16)
            qx, kx, vTx = _mla_proj(xs, mod_x, g1, wts, cos, sin, _row_tile(n, 256))
            qc, kc, vTc = _mla_proj(cs, mod_c, g1, wts, ones_c, zeros_c, _row_tile(nc, 256))
            ox = _flash(qx, [(kc, vTc), (kx, vTx)], _row_tile(n, 2048), 512)
            xs = _outproj(ox, w_o, xs, mod_x, _row_tile(n, 512))
            if ctx_later:
                oc = _flash(qc, [(kc, vTc)], _row_tile(nc, 256), 512)
                cs = _outproj(oc, w_o, cs, mod_c, _row_tile(nc, 256))
        else:
            xs = _fourier(xs, mod_x, g1, j, w_fno)
            if ctx_later:
                cs = _fourier(cs, mod_c, g1, j, w_fno)
        xs = _ffn(xs, mod_x, g2, i, w_up, ffn_conv_w, conv_b, w_down, fin_g, _row_tile(n, 512), 512,
                  final=(i == DEPTH - 1))
        if ctx_later:
            cs = _ffn(cs, mod_c, g2, i, w_up, ffn_conv_w, conv_b, w_down, fin_g, _row_tile(nc, 256), 512,
                      final=False)
    return xs[None]
```

```python
import functools
import math

import jax
import jax.numpy as jnp
import numpy as np
from jax import lax
from jax.experimental import pallas as pl
from jax.experimental.pallas import tpu as pltpu

D_MODEL = 2048
DEPTH = 4
GRID_W = 64
N_MIXERS = 2
MIXER_MLA = 0
N_HEADS = 16
Q_LORA = 512
KV_LORA = 512
QK_NOPE = 128
QK_ROPE = 64
V_DIM = 128
ROPE_BASE = 10000.0
SM_SCALE = (QK_NOPE + QK_ROPE) ** -0.5
FNO_GROUPS = 8
FNO_GROUP_DIM = D_MODEL // FNO_GROUPS
D_FF = 5632
NORM_EPS = 1e-6
N_MOD = 6

LANES = 128
SUBLANES = 8
BF16_ROWS = 16
FLASH_RING = 4
FLASH_AHEAD = 2
NORM_ROWS = BF16_ROWS
NORM_UNROLL = 8
FFN_COLS = 256
VMEM_LIMIT_BYTES = 56 * 1024 * 1024

HEAD_PAD = 2 * LANES
VT_ROWS = V_DIM + SUBLANES
Q_SCALE = SM_SCALE * math.log2(math.e)
FFT_N2 = 128
FFT_B2 = SUBLANES
FFT_B1 = BF16_ROWS

BF16 = jnp.bfloat16
F32 = jnp.float32
HIGHEST = lax.Precision.HIGHEST

SH1, SC1, G1, SH2, SC2, G2 = range(6)


def _params(*sem):
    return pltpu.CompilerParams(dimension_semantics=sem, vmem_limit_bytes=VMEM_LIMIT_BYTES)


def _const_spec(shape):
    nd = len(shape)
    return pl.BlockSpec(shape, lambda *_: (0,) * nd, pipeline_mode=pl.Buffered(1))


def _dot(a, b):
    return jnp.dot(a, b, preferred_element_type=F32)


def _dot_hi(a, b):
    return jnp.dot(a, b, preferred_element_type=F32, precision=HIGHEST)


def _rms(x, g):
    return x * lax.rsqrt(jnp.mean(x * x, axis=-1, keepdims=True) + NORM_EPS) * g


def _norm_mod(x, g, shift, scale):
    return _rms(x, g) * (1.0 + scale) + shift


def _ada_kernel(condT_ref, w_ref, b_ref, o_ref, acc_ref):
    k = pl.program_id(2)
    tk, tn = w_ref.shape[1], w_ref.shape[2]
    n_rows = acc_ref.shape[0]

    @pl.when(k == 0)
    def _():
        acc_ref[...] = jnp.zeros_like(acc_ref)

    w = w_ref[0]
    ct = condT_ref[pl.ds(pl.multiple_of(k * tk, tk), tk), :]
    s = ct * (1.0 / (1.0 + jnp.exp(-ct)))
    for r in range(n_rows):
        prod = w * s[:, r:r + 1]
        acc_ref[r] += prod.reshape(tk // SUBLANES, SUBLANES, tn).sum(axis=0)

    @pl.when(k == pl.num_programs(2) - 1)
    def _():
        for r in range(n_rows):
            o_ref[0, r:r + 1, :] = acc_ref[r].sum(axis=0, keepdims=True) + b_ref[0]


def _ada_all(cond, ada_w, ada_b):
    n_rows, d = cond.shape
    depth, _, nout = ada_w.shape
    tk, tn = 1024, 2048
    out = pl.pallas_call(
        _ada_kernel,
        grid=(depth, nout // tn, d // tk),
        in_specs=[
            pl.BlockSpec((d, n_rows), lambda l, j, k: (0, 0)),
            pl.BlockSpec((1, tk, tn), lambda l, j, k: (l, k, j)),
            pl.BlockSpec((1, 1, tn), lambda l, j, k: (l, 0, j)),
        ],
        out_specs=pl.BlockSpec((1, n_rows, tn), lambda l, j, k: (l, 0, j)),
        out_shape=jax.ShapeDtypeStruct((depth, n_rows, nout), F32),
        scratch_shapes=[pltpu.VMEM((n_rows, SUBLANES, tn), F32)],
        compiler_params=_params("arbitrary", "arbitrary", "arbitrary"),
        name="ada_mod",
    )(cond.T, ada_w, ada_b.reshape(depth, 1, nout))
    return out.reshape(depth, n_rows, N_MOD, d)


def _mla_proj_kernel(x_ref, mod_ref, g1_ref, wd_ref, gq_ref, gkv_ref, wq_ref, wk_ref, wvT_ref,
                     cos_ref, sin_ref, q_ref, k_ref, vT_ref):
    hd = N_HEADS * LANES
    h = _norm_mod(x_ref[...], g1_ref[...], mod_ref[SH1:SH1 + 1], mod_ref[SC1:SC1 + 1])
    d = _dot(h.astype(BF16), wd_ref[...])
    cq = _rms(d[:, :Q_LORA], gq_ref[...]).astype(BF16)
    ckv = _rms(d[:, Q_LORA:Q_LORA + KV_LORA], gkv_ref[...]).astype(BF16)
    cos = cos_ref[...]
    sin = sin_ref[...]
    base = Q_LORA + KV_LORA
    kr = (d[:, base:base + LANES] * cos + d[:, base + LANES:base + 2 * LANES] * sin).astype(BF16)
    qall = _dot(cq, wq_ref[...])
    kn = _dot(ckv, wk_ref[...])
    vT = lax.dot_general(wvT_ref[...], ckv, (((1,), (1,)), ((), ())),
                         preferred_element_type=F32)
    tm = x_ref.shape[0]
    ones_row = (lax.broadcasted_iota(jnp.int32, (VT_ROWS - V_DIM, tm), 0) == 0).astype(BF16)
    for hh in range(N_HEADS):
        lo, hi = hh * LANES, (hh + 1) * LANES
        qr = qall[:, hd + lo:hd + hi] * cos + qall[:, 2 * hd + lo:2 * hd + hi] * sin
        q_ref[hh, :, 0:LANES] = (qall[:, lo:hi] * Q_SCALE).astype(BF16)
        q_ref[hh, :, LANES:HEAD_PAD] = (qr * Q_SCALE).astype(BF16)
        k_ref[hh, :, 0:LANES] = kn[:, lo:hi].astype(BF16)
        k_ref[hh, :, LANES:HEAD_PAD] = kr
        vT_ref[hh, 0:V_DIM, :] = vT[lo:hi, :].astype(BF16)
        vT_ref[hh, V_DIM:VT_ROWS, :] = ones_row


def _mla_proj(x, mod, g1, wts, cos, sin, tm):
    n, d = x.shape
    wd, gq, gkv, wq, wk, wvT = wts
    return pl.pallas_call(
        _mla_proj_kernel,
        grid=(n // tm,),
        in_specs=[
            pl.BlockSpec((tm, d), lambda i: (i, 0)),
            _const_spec(mod.shape), _const_spec(g1.shape), _const_spec(wd.shape),
            _const_spec(gq.shape), _const_spec(gkv.shape), _const_spec(wq.shape),
            _const_spec(wk.shape), _const_spec(wvT.shape),
            pl.BlockSpec((tm, LANES), lambda i: (i, 0)),
            pl.BlockSpec((tm, LANES), lambda i: (i, 0)),
        ],
        out_specs=[
            pl.BlockSpec((N_HEADS, tm, HEAD_PAD), lambda i: (0, i, 0)),
            pl.BlockSpec((N_HEADS, tm, HEAD_PAD), lambda i: (0, i, 0)),
            pl.BlockSpec((N_HEADS, VT_ROWS, tm), lambda i: (0, 0, i)),
        ],
        out_shape=[
            jax.ShapeDtypeStruct((N_HEADS, n, HEAD_PAD), BF16),
            jax.ShapeDtypeStruct((N_HEADS, n, HEAD_PAD), BF16),
            jax.ShapeDtypeStruct((N_HEADS, VT_ROWS, n), BF16),
        ],
        compiler_params=_params("arbitrary"),
        name="mla_proj",
    )(x, mod, g1, wd, gq, gkv, wq, wk, wvT, cos, sin)


def _flash_kernel(*refs, seg_lens, tk):
    nseg = len(seg_lens)
    q_ref = refs[0]
    kv_refs = refs[1:1 + 2 * nseg]
    o_ref = refs[1 + 2 * nseg]
    m_ref, acc_ref, s_ref = refs[2 + 2 * nseg:]
    ring = s_ref.shape[0]
    q = q_ref[0]
    m_ref[...] = jnp.full_like(m_ref, -jnp.inf)
    acc_ref[...] = jnp.zeros_like(acc_ref)

    def chunk(c, ck):
        return pl.ds(c * ck if isinstance(c, int) else pl.multiple_of(c * ck, ck), ck)

    def scores(k_ref, c, ck):
        k = k_ref[0, chunk(c, ck), :]
        return lax.dot_general(k, q, (((1,), (1,)), ((), ())), preferred_element_type=F32)

    def update(s, vT_ref, c, ck):
        m_old = m_ref[...]
        m_new = jnp.maximum(m_old, jnp.max(s, axis=0, keepdims=True))
        p = jnp.exp2(s - m_new).astype(BF16)
        vT = vT_ref[0, :, chunk(c, ck)]
        acc_ref[...] = jnp.exp2(m_old - m_new) * acc_ref[...] + _dot(vT, p)
        m_ref[...] = m_new

    segs = []
    for si in range(nseg):
        ck = min(tk, seg_lens[si])
        segs.append((kv_refs[2 * si], kv_refs[2 * si + 1], ck, seg_lens[si] // ck))
    is_piped = lambda s: s[3] % ring == 0 and s[3] >= 2 * ring
    piped = [s for s in segs if is_piped(s)]
    assert len(piped) <= 1 and FLASH_AHEAD < ring
    short = [(seg, c) for seg in segs if not is_piped(seg) for c in range(seg[3])]
    s_short = [scores(seg[0], c, seg[2]) for seg, c in short]
    if piped:
        assert piped[0][2] == s_ref.shape[1]
        for a in range(FLASH_AHEAD):
            s_ref[a] = scores(piped[0][0], a, piped[0][2])
    for (seg, c), s in zip(short, s_short):
        update(s, seg[1], c, seg[2])
    for k_ref, vT_ref, ck, nch in piped:
        def step(c, u, prefetch):
            if prefetch:
                s_ref[(u + FLASH_AHEAD) % ring] = scores(k_ref, c + FLASH_AHEAD, ck)
            update(s_ref[u], vT_ref, c, ck)

        def body(t, carry):
            for u in range(ring):
                step(t * ring + u, u, True)
            return carry

        lax.fori_loop(0, nch // ring - 1, body, 0)
        for u in range(ring):
            c = nch - ring + u
            step(c, u, c + FLASH_AHEAD < nch)

    acc = acc_ref[...]
    o = acc[:V_DIM] * (1.0 / acc[V_DIM:V_DIM + 1])
    o_ref[...] = o.T.astype(o_ref.dtype)


def _flash(q, kvs, tq, tk):
    nh, n, _ = q.shape
    seg_lens = tuple(k.shape[1] for k, _ in kvs)
    in_specs = [pl.BlockSpec((1, tq, HEAD_PAD), lambda h, i: (h, i, 0))]
    args = [q]
    for k, vT in kvs:
        nk = k.shape[1]
        in_specs.append(pl.BlockSpec((1, nk, HEAD_PAD), lambda h, i: (h, 0, 0)))
        in_specs.append(pl.BlockSpec((1, VT_ROWS, nk), lambda h, i: (h, 0, 0)))
        args += [k, vT]
    ck = min(tk, max(seg_lens))
    return pl.pallas_call(
        functools.partial(_flash_kernel, seg_lens=seg_lens, tk=tk),
        grid=(nh, n // tq),
        in_specs=in_specs,
        out_specs=pl.BlockSpec((tq, V_DIM), lambda h, i: (i, h)),
        out_shape=jax.ShapeDtypeStruct((n, nh * V_DIM), BF16),
        scratch_shapes=[pltpu.VMEM((1, tq), F32), pltpu.VMEM((VT_ROWS, tq), F32),
                        pltpu.VMEM((FLASH_RING, ck, tq), F32)],
        compiler_params=_params("arbitrary", "arbitrary"),
        name="mla_flash",
    )(*args)


def _outproj_kernel(o_ref, w_ref, x_ref, mod_ref, out_ref):
    out_ref[...] = x_ref[...] + mod_ref[G1:G1 + 1] * _dot(o_ref[...], w_ref[...])


def _outproj(o, w, x, mod, tm):
    n, d = x.shape
    return pl.pallas_call(
        _outproj_kernel,
        grid=(n // tm,),
        in_specs=[
            pl.BlockSpec((tm, o.shape[1]), lambda i: (i, 0)),
            _const_spec(w.shape),
            pl.BlockSpec((tm, d), lambda i: (i, 0)),
            _const_spec(mod.shape),
        ],
        out_specs=pl.BlockSpec((tm, d), lambda i: (i, 0)),
        out_shape=jax.ShapeDtypeStruct((n, d), F32),
        compiler_params=_params("arbitrary"),
        name="mla_outproj",
    )(o, w, x, mod)


def _ffn_kernel(xp_ref, x_ref, xn_ref, mod_ref, g2_ref, wuv_ref, wug_ref, cwv_ref, cwg_ref,
                cbv_ref, cbg_ref, wd_ref, fg_ref, o_ref, h_ref, u_ref, *, tm, final):
    i = pl.program_id(0)
    f = pl.program_id(1)
    halo = BF16_ROWS
    rows = tm + 2 * halo

    @pl.when(f == 0)
    def _():
        g, sh, sc = g2_ref[...], mod_ref[SH2:SH2 + 1], mod_ref[SC2:SC2 + 1]
        hp = jnp.where(i > 0, _norm_mod(xp_ref[...], g, sh, sc), 0.0)
        hn = jnp.where(i < pl.num_programs(0) - 1, _norm_mod(xn_ref[...], g, sh, sc), 0.0)
        h_ref[0:halo] = hp.astype(BF16)
        h_ref[halo + tm:rows] = hn.astype(BF16)

        def norm_rows(r, carry):
            r0 = pl.multiple_of(r * NORM_ROWS, NORM_ROWS)
            xr = x_ref[pl.ds(r0, NORM_ROWS), :]
            h_ref[pl.ds(halo + r0, NORM_ROWS), :] = _norm_mod(xr, g, sh, sc).astype(BF16)
            return carry

        lax.fori_loop(0, tm // NORM_ROWS, norm_rows, 0, unroll=8)
        o_ref[...] = jnp.zeros_like(o_ref)

    hh = h_ref[...]
    tf = wd_ref.shape[0]
    tc = min(tf, FFN_COLS)

    def conv(slot, cw_ref, cb_ref, cols):
        cw = cw_ref[:, cols]
        return (u_ref[slot, halo - 1:halo - 1 + tm, :] * cw[0:1]
                + u_ref[slot, halo:halo + tm, :] * cw[1:2]
                + u_ref[slot, halo + 1:halo + 1 + tm, :] * cw[2:3] + cb_ref[:, cols])

    groups = [slice(c0, c0 + tc) for c0 in range(0, tf, tc)]
    for gi, cols in enumerate(groups):
        u_ref[2 * gi] = _dot(hh, wuv_ref[:, cols])
        u_ref[2 * gi + 1] = _dot(hh, wug_ref[:, cols])
    for gi, cols in enumerate(groups):
        val = conv(2 * gi, cwv_ref, cbv_ref, cols)
        gate = conv(2 * gi + 1, cwg_ref, cbg_ref, cols)
        act = (val * (gate * (1.0 / (1.0 + jnp.exp(-gate))))).astype(BF16)
        o_ref[...] += _dot(act, wd_ref[cols, :])

    @pl.when(f == pl.num_programs(1) - 1)
    def _():
        gate2, fg = mod_ref[G2:G2 + 1], fg_ref[...]

        def finish_rows(r, carry):
            base = pl.multiple_of(r * (NORM_UNROLL * NORM_ROWS), NORM_UNROLL * NORM_ROWS)
            groups = [pl.ds(base + u * NORM_ROWS, NORM_ROWS) for u in range(NORM_UNROLL)]
            resid = lambda rr: x_ref[rr, :] + gate2 * o_ref[rr, :]
            if final:
                inv = [lax.rsqrt(jnp.mean(jnp.square(resid(rr)), axis=-1, keepdims=True) + NORM_EPS)
                       for rr in groups]
                for rr, r_inv in zip(groups, inv):
                    o_ref[rr, :] = resid(rr) * r_inv * fg
            else:
                for rr in groups:
                    o_ref[rr, :] = resid(rr)
            return carry

        lax.fori_loop(0, tm // (NORM_UNROLL * NORM_ROWS), finish_rows, 0)


def _ffn(x, mod, g2, layer, w_up, conv_w, conv_b, w_down, final_g, tm, tf, final):
    n, d = x.shape
    dff = w_down.shape[1]
    nf = dff // tf
    halo = BF16_ROWS
    hb = tm // halo
    last = n // halo - 1
    return pl.pallas_call(
        functools.partial(_ffn_kernel, tm=tm, final=final),
        grid=(n // tm, nf),
        in_specs=[
            pl.BlockSpec((halo, d), lambda i, f: (jnp.maximum(i * hb - 1, 0), 0)),
            pl.BlockSpec((tm, d), lambda i, f: (i, 0), pipeline_mode=pl.Buffered(1)),
            pl.BlockSpec((halo, d), lambda i, f: (jnp.minimum((i + 1) * hb, last), 0)),
            _const_spec(mod.shape), _const_spec(g2.shape),
            pl.BlockSpec((None, d, tf), lambda i, f: (layer, 0, f)),
            pl.BlockSpec((None, d, tf), lambda i, f: (layer, 0, nf + f)),
            pl.BlockSpec((None, 3, tf), lambda i, f: (layer, 0, f)),
            pl.BlockSpec((None, 3, tf), lambda i, f: (layer, 0, nf + f)),
            pl.BlockSpec((None, 1, tf), lambda i, f: (layer, 0, f)),
            pl.BlockSpec((None, 1, tf), lambda i, f: (layer, 0, nf + f)),
            pl.BlockSpec((None, tf, d), lambda i, f: (layer, f, 0)),
            _const_spec(final_g.shape),
        ],
        out_specs=pl.BlockSpec((tm, d), lambda i, f: (i, 0)),
        out_shape=jax.ShapeDtypeStruct((n, d), F32),
        scratch_shapes=[pltpu.VMEM((tm + 2 * halo, d), BF16),
                        pltpu.VMEM((2 * (tf // min(tf, FFN_COLS)), tm + 2 * halo, min(tf, FFN_COLS)), F32)],
        compiler_params=_params("arbitrary", "arbitrary"),
        name="conv_ffn",
    )(x, x, x, mod, g2, w_up, w_up, conv_w, conv_w, conv_b, conv_b, w_down, final_g)


def _channel_dft(xr, xi, cc_ref, sc_ref, dot):
    parts = []
    for g in range(FNO_GROUPS):
        lo, hi = g * FNO_GROUP_DIM, (g + 1) * FNO_GROUP_DIM
        parts.append(dot(xr[:, lo:hi], cc_ref[...]) + dot(xi[:, lo:hi], sc_ref[...]))
    return jnp.concatenate(parts, axis=1)


def _fft_stage1_kernel(x_ref, mod_ref, g1_ref, kc_ref, ks_ref, yr_ref, yi_ref):
    n1, b1, d = x_ref.shape
    yr, yi = [], []
    for j0 in range(0, b1, FFT_B2):
        x = x_ref[:, j0:j0 + FFT_B2, :].reshape(n1 * FFT_B2, d)
        h = _norm_mod(x, g1_ref[...], mod_ref[SH1:SH1 + 1], mod_ref[SC1:SC1 + 1]).astype(BF16)
        yr.append(_dot(kc_ref[...], h).reshape(n1, FFT_B2, d))
        yi.append(_dot(ks_ref[...], h).reshape(n1, FFT_B2, d))
    yr_ref[...] = jnp.concatenate(yr, axis=1).astype(BF16)
    yi_ref[...] = jnp.concatenate(yi, axis=1).astype(BF16)


def _fft_stage2_kernel(yr_ref, yi_ref, m_ref, cc_ref, sc_ref, p_ref, w_ref, x_ref, mod_ref, o_ref,
                       f_ref, fp_ref):
    b2 = m_ref.shape[0]
    tn = o_ref.shape[2]

    @pl.when(pl.program_id(1) == 0)
    def _():
        for q in range(b2):
            rows = slice(q * FFT_N2, (q + 1) * FFT_N2)
            ys = jnp.concatenate([yr_ref[rows], yi_ref[rows]], axis=0)
            xs = _dot(m_ref[q], ys).astype(BF16)
            f_ref[rows] = _channel_dft(xs[:FFT_N2], xs[FFT_N2:], cc_ref, sc_ref, _dot).astype(BF16)
        fp_ref[...] = _dot(p_ref[...], f_ref[...]).astype(BF16)

    y = _dot(fp_ref[...], w_ref[...])
    o_ref[...] = x_ref[...] + mod_ref[G1:G1 + 1].reshape(1, 1, tn) * y.reshape(FFT_N2, b2, tn)


def _dft_small_kernel(x_ref, mod_ref, g1_ref, m_ref, cc_ref, sc_ref, w_ref, o_ref):
    n = x_ref.shape[0]
    x = x_ref[...]
    h = _norm_mod(x, g1_ref[...], mod_ref[SH1:SH1 + 1], mod_ref[SC1:SC1 + 1])
    xs = _dot_hi(m_ref[...], h)
    fmix = _channel_dft(xs[:n], xs[n:], cc_ref, sc_ref, _dot_hi).astype(BF16)
    o_ref[...] = x + mod_ref[G1:G1 + 1] * _dot(fmix, w_ref[...])


def _cos_sin(num, den, scale):
    ang = (2.0 * np.pi / den) * (num % den).astype(np.float64)
    return (np.cos(ang) * scale).astype(np.float32), (np.sin(ang) * scale).astype(np.float32)


def _dft_tables(n):
    gd = FNO_GROUP_DIM
    a = np.arange(gd, dtype=np.int64)
    cc, sc = _cos_sin(a[:, None] * a[None, :], gd, gd ** -0.5)
    if n % (FFT_N2 * FFT_B2) == 0 and n > 2 * FFT_N2:
        n1 = n // FFT_N2
        j1 = np.arange(n1, dtype=np.int64)
        c1, s1 = _cos_sin(j1[:, None] * j1[None, :], n1, n1 ** -0.5)
        eye = np.eye(FFT_B2, dtype=np.float32)
        k = j1[:, None, None] + n1 * np.arange(FFT_N2, dtype=np.int64)[None, :, None]
        j2 = np.arange(FFT_N2, dtype=np.int64)[None, None, :]
        mc, ms = _cos_sin(k * j2, n, FFT_N2 ** -0.5)
        m2 = np.concatenate([np.concatenate([mc, ms], axis=2),
                             np.concatenate([-ms, mc], axis=2)], axis=1)
        rows = np.arange(FFT_N2 * FFT_B2)
        perm = np.zeros((FFT_N2 * FFT_B2,) * 2, np.float32)
        perm[rows, (rows % FFT_B2) * FFT_N2 + rows // FFT_B2] = 1.0
        tabs = {"kc": np.kron(c1, eye), "ks": np.kron(-s1, eye), "m2": m2, "perm": perm,
                "cc": cc, "sc": sc}
        return {k_: jnp.asarray(v).astype(BF16) for k_, v in tabs.items()}
    j = np.arange(n, dtype=np.int64)
    c, s = _cos_sin(j[:, None] * j[None, :], n, n ** -0.5)
    tabs = {"m": np.concatenate([c, -s], axis=0), "cc": cc, "sc": sc}
    return {k_: jnp.asarray(v) for k_, v in tabs.items()}


def _fourier_long(x, mod, g1, layer, w, tabs, tn):
    n, d = x.shape
    n1 = n // FFT_N2
    nb = n1 * FFT_B2
    yr, yi = pl.pallas_call(
        _fft_stage1_kernel,
        grid=(FFT_N2 // FFT_B1,),
        in_specs=[
            pl.BlockSpec((n1, FFT_B1, d), lambda s: (0, s, 0)),
            _const_spec(mod.shape), _const_spec(g1.shape),
            _const_spec((nb, nb)), _const_spec((nb, nb)),
        ],
        out_specs=[pl.BlockSpec((n1, FFT_B1, d), lambda s: (0, s, 0))] * 2,
        out_shape=[jax.ShapeDtypeStruct((n1, FFT_N2, d), BF16)] * 2,
        compiler_params=_params("arbitrary"),
        name="fft_stage1",
    )(x.reshape(n1, FFT_N2, d), mod, g1, tabs["kc"], tabs["ks"])
    rows = FFT_N2 * FFT_B2
    out = pl.pallas_call(
        _fft_stage2_kernel,
        grid=(n1 // FFT_B2, d // tn),
        in_specs=[
            pl.BlockSpec((rows, d), lambda a, j: (a, 0)),
            pl.BlockSpec((rows, d), lambda a, j: (a, 0)),
            pl.BlockSpec((FFT_B2, 2 * FFT_N2, 2 * FFT_N2), lambda a, j: (a, 0, 0)),
            _const_spec(tabs["cc"].shape), _const_spec(tabs["sc"].shape),
            _const_spec(tabs["perm"].shape),
            pl.BlockSpec((None, d, tn), lambda a, j: (layer, 0, j)),
            pl.BlockSpec((FFT_N2, FFT_B2, tn), lambda a, j: (0, a, j)),
            pl.BlockSpec((N_MOD, tn), lambda a, j: (0, j)),
        ],
        out_specs=pl.BlockSpec((FFT_N2, FFT_B2, tn), lambda a, j: (0, a, j)),
        out_shape=jax.ShapeDtypeStruct((FFT_N2, n1, d), F32),
        scratch_shapes=[pltpu.VMEM((rows, d), BF16), pltpu.VMEM((rows, d), BF16)],
        compiler_params=_params("arbitrary", "arbitrary"),
        name="fft_stage2",
    )(yr.reshape(n, d), yi.reshape(n, d), tabs["m2"], tabs["cc"], tabs["sc"], tabs["perm"], w,
      x.reshape(FFT_N2, n1, d), mod)
    return out.reshape(n, d)


def _fourier_short(x, mod, g1, layer, w, tabs):
    n, d = x.shape
    return pl.pallas_call(
        _dft_small_kernel,
        grid=(1,),
        in_specs=[_const_spec(a.shape) for a in (x, mod, g1, tabs["m"], tabs["cc"], tabs["sc"])]
        + [pl.BlockSpec((None, d, d), lambda i: (layer, 0, 0))],
        out_specs=pl.BlockSpec((n, d), lambda i: (0, 0)),
        out_shape=jax.ShapeDtypeStruct((n, d), F32),
        compiler_params=_params("arbitrary"),
        name="dft_ctx",
    )(x, mod, g1, tabs["m"], tabs["cc"], tabs["sc"], w)


def _fourier(x, mod, g1, layer, w):
    tabs = _dft_tables(x.shape[0])
    if "m2" in tabs:
        return _fourier_long(x, mod, g1, layer, w, tabs, 512)
    return _fourier_short(x, mod, g1, layer, w, tabs)


def _rot_cols(w):
    q = QK_ROPE // 4
    a1, a2, b1, b2 = w[..., :q], w[..., q:2 * q], w[..., 2 * q:3 * q], w[..., 3 * q:]
    return jnp.concatenate([-a2, a1, -b2, b1], axis=-1)


def _pad_lanes(w):
    return jnp.pad(w, [(0, 0)] * (w.ndim - 1) + [(0, LANES - w.shape[-1])])


def _mla_weights(w_dqkv, g_q, g_kv, w_uq, w_ukv):
    hd = N_HEADS * LANES
    kr = w_dqkv[:, Q_LORA + KV_LORA:]
    wd = jnp.concatenate([w_dqkv[:, :Q_LORA + KV_LORA], _pad_lanes(kr), _pad_lanes(_rot_cols(kr))],
                         axis=1).astype(BF16)
    q3 = w_uq.reshape(Q_LORA, N_HEADS, QK_NOPE + QK_ROPE)
    qr = q3[:, :, QK_NOPE:]
    wq = jnp.concatenate([q3[:, :, :QK_NOPE].reshape(Q_LORA, hd),
                          _pad_lanes(qr).reshape(Q_LORA, hd),
                          _pad_lanes(_rot_cols(qr)).reshape(Q_LORA, hd)], axis=1).astype(BF16)
    kv3 = w_ukv.reshape(KV_LORA, N_HEADS, QK_NOPE + V_DIM)
    wk = kv3[:, :, :QK_NOPE].reshape(KV_LORA, hd).astype(BF16)
    wvT = kv3[:, :, QK_NOPE:].reshape(KV_LORA, N_HEADS * V_DIM).T.astype(BF16)
    return wd, g_q.reshape(1, -1), g_kv.reshape(1, -1), wq, wk, wvT


def _rope_tables(n):
    rows = n // GRID_W
    r, col = jnp.meshgrid(jnp.arange(rows, dtype=F32), jnp.arange(GRID_W, dtype=F32), indexing="ij")
    half = QK_ROPE // 2
    inv_freq = jnp.power(ROPE_BASE, -jnp.arange(0, half, 2, dtype=F32) / half)
    ang_r = r.reshape(-1)[:, None] * inv_freq
    ang_c = col.reshape(-1)[:, None] * inv_freq
    ang = jnp.concatenate([ang_r, ang_r, ang_c, ang_c], axis=-1)
    return _pad_lanes(jnp.cos(ang)), _pad_lanes(jnp.sin(ang))


def _row_tile(n, pref):
    return pref if n % pref == 0 else n


def kernel(x, c, ctx, c_ctx, ada_w, ada_b, norm1_g, norm2_g, mla_w_dqkv, mla_q_norm_g, mla_kv_norm_g,
           mla_w_uq, mla_w_ukv, mla_w_o, fno_w, ffn_w_up, ffn_conv_w, ffn_conv_b, ffn_w_down,
           final_norm_g):
    assert x.shape[0] == 1 and c.shape[0] == 1 and ctx.shape[0] == 1
    xs = x[0]
    cs = ctx[0]
    n, nc = xs.shape[0], cs.shape[0]
    mods = _ada_all(jnp.concatenate([c, c_ctx[None, :]], axis=0), ada_w, ada_b)
    cos, sin = _rope_tables(n)
    ones_c = _pad_lanes(jnp.ones((nc, QK_ROPE), F32))
    zeros_c = jnp.zeros((nc, LANES), F32)
    fin_g = final_norm_g.reshape(1, -1)
    w_up = ffn_w_up.astype(BF16)
    w_down = ffn_w_down.astype(BF16)
    w_fno = fno_w.astype(BF16)
    conv_b = ffn_conv_b.reshape(DEPTH, 1, -1)

    for i in range(DEPTH):
        kind = i % N_MIXERS
        j = i // N_MIXERS
        ctx_later = any(l % N_MIXERS == MIXER_MLA for l in range(i + 1, DEPTH))
        mod_x, mod_c = mods[i, 0], mods[i, 1]
        g1 = norm1_g[i].reshape(1, -1)
        g2 = norm2_g[i].reshape(1, -1)
        if kind == MIXER_MLA:
            wts = _mla_weights(mla_w_dqkv[j], mla_q_norm_g[j], mla_kv_norm_g[j], mla_w_uq[j], mla_w_ukv[j])
            w_o = mla_w_o[j].astype(BF16)
            qx, kx, vTx = _mla_proj(xs, mod_x, g1, wts, cos, sin, _row_tile(n, 256))
            qc, kc, vTc = _mla_proj(cs, mod_c, g1, wts, ones_c, zeros_c, _row_tile(nc, 256))
            ox = _flash(qx, [(kc, vTc), (kx, vTx)], _row_tile(n, 2048), 512)
            xs = _outproj(ox, w_o, xs, mod_x, _row_tile(n, 512))
            if ctx_later:
                oc = _flash(qc, [(kc, vTc)], _row_tile(nc, 256), 512)
                cs = _outproj(oc, w_o, cs, mod_c, _row_tile(nc, 256))
        else:
            xs = _fourier(xs, mod_x, g1, j, w_fno)
            if ctx_later:
                cs = _fourier(cs, mod_c, g1, j, w_fno)
        xs = _ffn(xs, mod_x, g2, i, w_up, ffn_conv_w, conv_b, w_down, fin_g, _row_tile(n, 1024), 512,
                  final=(i == DEPTH - 1))
        if ctx_later:
            cs = _ffn(cs, mod_c, g2, i, w_up, ffn_conv_w, conv_b, w_down, fin_g, _row_tile(nc, 256), 512,
                      final=False)
    return xs[None]
```

```python
import functools
import math

import jax
import jax.numpy as jnp
import numpy as np
from jax import lax
from jax.experimental import pallas as pl
from jax.experimental.pallas import tpu as pltpu

D_MODEL = 2048
DEPTH = 4
GRID_W = 64
N_MIXERS = 2
MIXER_MLA = 0
N_HEADS = 16
Q_LORA = 512
KV_LORA = 512
QK_NOPE = 128
QK_ROPE = 64
V_DIM = 128
ROPE_BASE = 10000.0
SM_SCALE = (QK_NOPE + QK_ROPE) ** -0.5
FNO_GROUPS = 8
FNO_GROUP_DIM = D_MODEL // FNO_GROUPS
D_FF = 5632
NORM_EPS = 1e-6
N_MOD = 6

LANES = 128
SUBLANES = 8
BF16_ROWS = 16
FLASH_RING = 4
FLASH_AHEAD = 2
NORM_ROWS = BF16_ROWS
NORM_UNROLL = 8
FFN_COLS = 256
VMEM_LIMIT_BYTES = 56 * 1024 * 1024

HEAD_PAD = 2 * LANES
assert QK_NOPE == LANES and 2 * QK_ROPE == LANES and N_HEADS % 2 == 0
VT_ROWS = V_DIM + SUBLANES
Q_SCALE = SM_SCALE * math.log2(math.e)
FFT_N2 = 128
FFT_B2 = SUBLANES
FFT_B1 = BF16_ROWS

BF16 = jnp.bfloat16
F32 = jnp.float32
HIGHEST = lax.Precision.HIGHEST

SH1, SC1, G1, SH2, SC2, G2 = range(6)


def _params(*sem):
    return pltpu.CompilerParams(dimension_semantics=sem, vmem_limit_bytes=VMEM_LIMIT_BYTES)


def _const_spec(shape):
    nd = len(shape)
    return pl.BlockSpec(shape, lambda *_: (0,) * nd, pipeline_mode=pl.Buffered(1))


def _dot(a, b):
    return jnp.dot(a, b, preferred_element_type=F32)


def _dot_hi(a, b):
    return jnp.dot(a, b, preferred_element_type=F32, precision=HIGHEST)


def _rms(x, g):
    return x * lax.rsqrt(jnp.mean(x * x, axis=-1, keepdims=True) + NORM_EPS) * g


def _norm_mod(x, g, shift, scale):
    return _rms(x, g) * (1.0 + scale) + shift


def _ada_kernel(condT_ref, w_ref, b_ref, o_ref, acc_ref):
    k = pl.program_id(2)
    tk, tn = w_ref.shape[1], w_ref.shape[2]
    n_rows = acc_ref.shape[0]

    @pl.when(k == 0)
    def _():
        acc_ref[...] = jnp.zeros_like(acc_ref)

    w = w_ref[0]
    ct = condT_ref[pl.ds(pl.multiple_of(k * tk, tk), tk), :]
    s = ct * (1.0 / (1.0 + jnp.exp(-ct)))
    for r in range(n_rows):
        prod = w * s[:, r:r + 1]
        acc_ref[r] += prod.reshape(tk // SUBLANES, SUBLANES, tn).sum(axis=0)

    @pl.when(k == pl.num_programs(2) - 1)
    def _():
        for r in range(n_rows):
            o_ref[0, r:r + 1, :] = acc_ref[r].sum(axis=0, keepdims=True) + b_ref[0]


def _ada_all(cond, ada_w, ada_b):
    n_rows, d = cond.shape
    depth, _, nout = ada_w.shape
    tk, tn = 1024, 2048
    out = pl.pallas_call(
        _ada_kernel,
        grid=(depth, nout // tn, d // tk),
        in_specs=[
            pl.BlockSpec((d, n_rows), lambda l, j, k: (0, 0)),
            pl.BlockSpec((1, tk, tn), lambda l, j, k: (l, k, j)),
            pl.BlockSpec((1, 1, tn), lambda l, j, k: (l, 0, j)),
        ],
        out_specs=pl.BlockSpec((1, n_rows, tn), lambda l, j, k: (l, 0, j)),
        out_shape=jax.ShapeDtypeStruct((depth, n_rows, nout), F32),
        scratch_shapes=[pltpu.VMEM((n_rows, SUBLANES, tn), F32)],
        compiler_params=_params("arbitrary", "arbitrary", "arbitrary"),
        name="ada_mod",
    )(cond.T, ada_w, ada_b.reshape(depth, 1, nout))
    return out.reshape(depth, n_rows, N_MOD, d)


def _mla_proj_kernel(x_ref, mod_ref, g1_ref, wd_ref, gq_ref, gkv_ref, wq_ref, wk_ref, wvT_ref,
                     cos_ref, sin_ref, q_ref, k_ref, vT_ref):
    hd = N_HEADS * LANES
    h = _norm_mod(x_ref[...], g1_ref[...], mod_ref[SH1:SH1 + 1], mod_ref[SC1:SC1 + 1])
    d = _dot(h.astype(BF16), wd_ref[...])
    cq = _rms(d[:, :Q_LORA], gq_ref[...]).astype(BF16)
    ckv = _rms(d[:, Q_LORA:Q_LORA + KV_LORA], gkv_ref[...]).astype(BF16)
    cos = cos_ref[...]
    sin = sin_ref[...]
    base = Q_LORA + KV_LORA
    kr = (d[:, base:base + LANES] * cos + d[:, base + LANES:base + 2 * LANES] * sin).astype(BF16)
    hr = N_HEADS * QK_ROPE
    qall = _dot(cq, wq_ref[...])
    kn = _dot(ckv, wk_ref[...])
    vT = lax.dot_general(wvT_ref[...], ckv, (((1,), (1,)), ((), ())),
                         preferred_element_type=F32)
    tm = x_ref.shape[0]
    ones_row = (lax.broadcasted_iota(jnp.int32, (VT_ROWS - V_DIM, tm), 0) == 0).astype(BF16)
    low_half = lax.broadcasted_iota(jnp.int32, (tm, LANES), 1) < QK_ROPE
    for hh in range(N_HEADS):
        lo, hi = hh * LANES, (hh + 1) * LANES
        if hh % 2 == 0:
            po = hd + (hh // 2) * LANES
            pair = (qall[:, po:po + LANES] * cos + qall[:, hr + po:hr + po + LANES] * sin) * Q_SCALE
        own = pair if hh % 2 == 0 else pltpu.roll(pair, QK_ROPE, 1)
        qr = jnp.where(low_half, own, 0.0)
        q_ref[hh, :, 0:LANES] = (qall[:, lo:hi] * Q_SCALE).astype(BF16)
        q_ref[hh, :, LANES:HEAD_PAD] = qr.astype(BF16)
        k_ref[hh, :, 0:LANES] = kn[:, lo:hi].astype(BF16)
        k_ref[hh, :, LANES:HEAD_PAD] = kr
        vT_ref[hh, 0:V_DIM, :] = vT[lo:hi, :].astype(BF16)
        vT_ref[hh, V_DIM:VT_ROWS, :] = ones_row


def _mla_proj(x, mod, g1, wts, cos, sin, tm):
    n, d = x.shape
    wd, gq, gkv, wq, wk, wvT = wts
    return pl.pallas_call(
        _mla_proj_kernel,
        grid=(n // tm,),
        in_specs=[
            pl.BlockSpec((tm, d), lambda i: (i, 0)),
            _const_spec(mod.shape), _const_spec(g1.shape), _const_spec(wd.shape),
            _const_spec(gq.shape), _const_spec(gkv.shape), _const_spec(wq.shape),
            _const_spec(wk.shape), _const_spec(wvT.shape),
            pl.BlockSpec((tm, LANES), lambda i: (i, 0)),
            pl.BlockSpec((tm, LANES), lambda i: (i, 0)),
        ],
        out_specs=[
            pl.BlockSpec((N_HEADS, tm, HEAD_PAD), lambda i: (0, i, 0)),
            pl.BlockSpec((N_HEADS, tm, HEAD_PAD), lambda i: (0, i, 0)),
            pl.BlockSpec((N_HEADS, VT_ROWS, tm), lambda i: (0, 0, i)),
        ],
        out_shape=[
            jax.ShapeDtypeStruct((N_HEADS, n, HEAD_PAD), BF16),
            jax.ShapeDtypeStruct((N_HEADS, n, HEAD_PAD), BF16),
            jax.ShapeDtypeStruct((N_HEADS, VT_ROWS, n), BF16),
        ],
        compiler_params=_params("arbitrary"),
        name="mla_proj",
    )(x, mod, g1, wd, gq, gkv, wq, wk, wvT, cos, sin)


def _flash_kernel(*refs, seg_lens, tk):
    nseg = len(seg_lens)
    q_ref = refs[0]
    kv_refs = refs[1:1 + 2 * nseg]
    o_ref = refs[1 + 2 * nseg]
    m_ref, acc_ref, s_ref = refs[2 + 2 * nseg:]
    ring = s_ref.shape[0]
    q = q_ref[0]
    m_ref[...] = jnp.full_like(m_ref, -jnp.inf)
    acc_ref[...] = jnp.zeros_like(acc_ref)

    def chunk(c, ck):
        return pl.ds(c * ck if isinstance(c, int) else pl.multiple_of(c * ck, ck), ck)

    def scores(k_ref, c, ck):
        k = k_ref[0, chunk(c, ck), :]
        return lax.dot_general(k, q, (((1,), (1,)), ((), ())), preferred_element_type=F32)

    def update(s, vT_ref, c, ck):
        m_old = m_ref[...]
        m_new = jnp.maximum(m_old, jnp.max(s, axis=0, keepdims=True))
        p = jnp.exp2(s - m_new).astype(BF16)
        vT = vT_ref[0, :, chunk(c, ck)]
        acc_ref[...] = jnp.exp2(m_old - m_new) * acc_ref[...] + _dot(vT, p)
        m_ref[...] = m_new

    segs = []
    for si in range(nseg):
        ck = min(tk, seg_lens[si])
        segs.append((kv_refs[2 * si], kv_refs[2 * si + 1], ck, seg_lens[si] // ck))
    is_piped = lambda s: s[3] % ring == 0 and s[3] >= 2 * ring
    piped = [s for s in segs if is_piped(s)]
    assert len(piped) <= 1 and FLASH_AHEAD < ring
    short = [(seg, c) for seg in segs if not is_piped(seg) for c in range(seg[3])]
    s_short = [scores(seg[0], c, seg[2]) for seg, c in short]
    if piped:
        assert piped[0][2] == s_ref.shape[1]
        for a in range(FLASH_AHEAD):
            s_ref[a] = scores(piped[0][0], a, piped[0][2])
    for (seg, c), s in zip(short, s_short):
        update(s, seg[1], c, seg[2])
    for k_ref, vT_ref, ck, nch in piped:
        def step(c, u, prefetch):
            if prefetch:
                s_ref[(u + FLASH_AHEAD) % ring] = scores(k_ref, c + FLASH_AHEAD, ck)
            update(s_ref[u], vT_ref, c, ck)

        def body(t, carry):
            for u in range(ring):
                step(t * ring + u, u, True)
            return carry

        lax.fori_loop(0, nch // ring - 1, body, 0)
        for u in range(ring):
            c = nch - ring + u
            step(c, u, c + FLASH_AHEAD < nch)

    acc = acc_ref[...]
    o = acc[:V_DIM] * (1.0 / acc[V_DIM:V_DIM + 1])
    o_ref[...] = o.T.astype(o_ref.dtype)


def _flash(q, kvs, tq, tk):
    nh, n, _ = q.shape
    seg_lens = tuple(k.shape[1] for k, _ in kvs)
    in_specs = [pl.BlockSpec((1, tq, HEAD_PAD), lambda h, i: (h, i, 0))]
    args = [q]
    for k, vT in kvs:
        nk = k.shape[1]
        in_specs.append(pl.BlockSpec((1, nk, HEAD_PAD), lambda h, i: (h, 0, 0)))
        in_specs.append(pl.BlockSpec((1, VT_ROWS, nk), lambda h, i: (h, 0, 0)))
        args += [k, vT]
    ck = min(tk, max(seg_lens))
    return pl.pallas_call(
        functools.partial(_flash_kernel, seg_lens=seg_lens, tk=tk),
        grid=(nh, n // tq),
        in_specs=in_specs,
        out_specs=pl.BlockSpec((tq, V_DIM), lambda h, i: (i, h)),
        out_shape=jax.ShapeDtypeStruct((n, nh * V_DIM), BF16),
        scratch_shapes=[pltpu.VMEM((1, tq), F32), pltpu.VMEM((VT_ROWS, tq), F32),
                        pltpu.VMEM((FLASH_RING, ck, tq), F32)],
        compiler_params=_params("arbitrary", "arbitrary"),
        name="mla_flash",
    )(*args)


def _outproj_kernel(o_ref, w_ref, x_ref, mod_ref, out_ref):
    out_ref[...] = x_ref[...] + mod_ref[G1:G1 + 1] * _dot(o_ref[...], w_ref[...])


def _outproj(o, w, x, mod, tm):
    n, d = x.shape
    return pl.pallas_call(
        _outproj_kernel,
        grid=(n // tm,),
        in_specs=[
            pl.BlockSpec((tm, o.shape[1]), lambda i: (i, 0)),
            _const_spec(w.shape),
            pl.BlockSpec((tm, d), lambda i: (i, 0)),
            _const_spec(mod.shape),
        ],
        out_specs=pl.BlockSpec((tm, d), lambda i: (i, 0)),
        out_shape=jax.ShapeDtypeStruct((n, d), F32),
        compiler_params=_params("arbitrary"),
        name="mla_outproj",
    )(o, w, x, mod)


def _ffn_kernel(xp_ref, x_ref, xn_ref, mod_ref, g2_ref, wuv_ref, wug_ref, cwv_ref, cwg_ref,
                cbv_ref, cbg_ref, wd_ref, fg_ref, o_ref, h_ref, u_ref, *, tm, final):
    i = pl.program_id(0)
    f = pl.program_id(1)
    halo = BF16_ROWS
    rows = tm + 2 * halo

    @pl.when(f == 0)
    def _():
        g, sh, sc = g2_ref[...], mod_ref[SH2:SH2 + 1], mod_ref[SC2:SC2 + 1]
        hp = jnp.where(i > 0, _norm_mod(xp_ref[...], g, sh, sc), 0.0)
        hn = jnp.where(i < pl.num_programs(0) - 1, _norm_mod(xn_ref[...], g, sh, sc), 0.0)
        h_ref[0:halo] = hp.astype(BF16)
        h_ref[halo + tm:rows] = hn.astype(BF16)

        def norm_rows(r, carry):
            r0 = pl.multiple_of(r * NORM_ROWS, NORM_ROWS)
            xr = x_ref[pl.ds(r0, NORM_ROWS), :]
            h_ref[pl.ds(halo + r0, NORM_ROWS), :] = _norm_mod(xr, g, sh, sc).astype(BF16)
            return carry

        lax.fori_loop(0, tm // NORM_ROWS, norm_rows, 0, unroll=8)
        o_ref[...] = jnp.zeros_like(o_ref)

    hh = h_ref[...]
    tf = wd_ref.shape[0]
    tc = min(tf, FFN_COLS)

    def conv(slot, cw_ref, cb_ref, cols):
        cw = cw_ref[:, cols]
        return (u_ref[slot, halo - 1:halo - 1 + tm, :] * cw[0:1]
                + u_ref[slot, halo:halo + tm, :] * cw[1:2]
                + u_ref[slot, halo + 1:halo + 1 + tm, :] * cw[2:3] + cb_ref[:, cols])

    groups = [slice(c0, c0 + tc) for c0 in range(0, tf, tc)]
    for gi, cols in enumerate(groups):
        u_ref[2 * gi] = _dot(hh, wuv_ref[:, cols])
        u_ref[2 * gi + 1] = _dot(hh, wug_ref[:, cols])
    for gi, cols in enumerate(groups):
        val = conv(2 * gi, cwv_ref, cbv_ref, cols)
        gate = conv(2 * gi + 1, cwg_ref, cbg_ref, cols)
        act = (val * (gate * (1.0 / (1.0 + jnp.exp(-gate))))).astype(BF16)
        o_ref[...] += _dot(act, wd_ref[cols, :])

    @pl.when(f == pl.num_programs(1) - 1)
    def _():
        gate2, fg = mod_ref[G2:G2 + 1], fg_ref[...]

        def finish_rows(r, carry):
            base = pl.multiple_of(r * (NORM_UNROLL * NORM_ROWS), NORM_UNROLL * NORM_ROWS)
            groups = [pl.ds(base + u * NORM_ROWS, NORM_ROWS) for u in range(NORM_UNROLL)]
            resid = lambda rr: x_ref[rr, :] + gate2 * o_ref[rr, :]
            if final:
                inv = [lax.rsqrt(jnp.mean(jnp.square(resid(rr)), axis=-1, keepdims=True) + NORM_EPS)
                       for rr in groups]
                for rr, r_inv in zip(groups, inv):
                    o_ref[rr, :] = resid(rr) * r_inv * fg
            else:
                for rr in groups:
                    o_ref[rr, :] = resid(rr)
            return carry

        lax.fori_loop(0, tm // (NORM_UNROLL * NORM_ROWS), finish_rows, 0)


def _ffn(x, mod, g2, layer, w_up, conv_w, conv_b, w_down, final_g, tm, tf, final):
    n, d = x.shape
    dff = w_down.shape[1]
    nf = dff // tf
    halo = BF16_ROWS
    hb = tm // halo
    last = n // halo - 1
    return pl.pallas_call(
        functools.partial(_ffn_kernel, tm=tm, final=final),
        grid=(n // tm, nf),
        in_specs=[
            pl.BlockSpec((halo, d), lambda i, f: (jnp.maximum(i * hb - 1, 0), 0)),
            pl.BlockSpec((tm, d), lambda i, f: (i, 0)),
            pl.BlockSpec((halo, d), lambda i, f: (jnp.minimum((i + 1) * hb, last), 0)),
            _const_spec(mod.shape), _const_spec(g2.shape),
            pl.BlockSpec((None, d, tf), lambda i, f: (layer, 0, f)),
            pl.BlockSpec((None, d, tf), lambda i, f: (layer, 0, nf + f)),
            pl.BlockSpec((None, 3, tf), lambda i, f: (layer, 0, f)),
            pl.BlockSpec((None, 3, tf), lambda i, f: (layer, 0, nf + f)),
            pl.BlockSpec((None, 1, tf), lambda i, f: (layer, 0, f)),
            pl.BlockSpec((None, 1, tf), lambda i, f: (layer, 0, nf + f)),
            pl.BlockSpec((None, tf, d), lambda i, f: (layer, f, 0)),
            _const_spec(final_g.shape),
        ],
        out_specs=pl.BlockSpec((tm, d), lambda i, f: (i, 0)),
        out_shape=jax.ShapeDtypeStruct((n, d), F32),
        scratch_shapes=[pltpu.VMEM((tm + 2 * halo, d), BF16),
                        pltpu.VMEM((2 * (tf // min(tf, FFN_COLS)), tm + 2 * halo, min(tf, FFN_COLS)), F32)],
        compiler_params=_params("arbitrary", "arbitrary"),
        name="conv_ffn",
    )(x, x, x, mod, g2, w_up, w_up, conv_w, conv_w, conv_b, conv_b, w_down, final_g)


def _channel_dft(xr, xi, cc_ref, sc_ref, dot):
    parts = []
    for g in range(FNO_GROUPS):
        lo, hi = g * FNO_GROUP_DIM, (g + 1) * FNO_GROUP_DIM
        parts.append(dot(xr[:, lo:hi], cc_ref[...]) + dot(xi[:, lo:hi], sc_ref[...]))
    return jnp.concatenate(parts, axis=1)


def _fft_stage1_kernel(x_ref, mod_ref, g1_ref, kc_ref, ks_ref, yr_ref, yi_ref):
    n1, b1, d = x_ref.shape
    yr, yi = [], []
    for j0 in range(0, b1, FFT_B2):
        x = x_ref[:, j0:j0 + FFT_B2, :].reshape(n1 * FFT_B2, d)
        h = _norm_mod(x, g1_ref[...], mod_ref[SH1:SH1 + 1], mod_ref[SC1:SC1 + 1]).astype(BF16)
        yr.append(_dot(kc_ref[...], h).reshape(n1, FFT_B2, d))
        yi.append(_dot(ks_ref[...], h).reshape(n1, FFT_B2, d))
    yr_ref[...] = jnp.concatenate(yr, axis=1).astype(BF16)
    yi_ref[...] = jnp.concatenate(yi, axis=1).astype(BF16)


def _fft_stage2_kernel(yr_ref, yi_ref, m_ref, cc_ref, sc_ref, p_ref, w_ref, x_ref, mod_ref, o_ref,
                       f_ref, fp_ref):
    b2 = m_ref.shape[0]
    tn = o_ref.shape[2]

    @pl.when(pl.program_id(1) == 0)
    def _():
        for q in range(b2):
            rows = slice(q * FFT_N2, (q + 1) * FFT_N2)
            ys = jnp.concatenate([yr_ref[rows], yi_ref[rows]], axis=0)
            xs = _dot(m_ref[q], ys).astype(BF16)
            f_ref[rows] = _channel_dft(xs[:FFT_N2], xs[FFT_N2:], cc_ref, sc_ref, _dot).astype(BF16)
        pr = p_ref.shape[0]
        kb = pr // b2
        for b in range(FFT_N2 // kb):
            g = jnp.concatenate([f_ref[q * FFT_N2 + b * kb:q * FFT_N2 + (b + 1) * kb, :]
                                 for q in range(b2)], axis=0)
            fp_ref[b * pr:(b + 1) * pr, :] = _dot(p_ref[...], g).astype(BF16)

    y = _dot(fp_ref[...], w_ref[...])
    o_ref[...] = x_ref[...] + mod_ref[G1:G1 + 1].reshape(1, 1, tn) * y.reshape(FFT_N2, b2, tn)


def _dft_small_kernel(x_ref, mod_ref, g1_ref, m_ref, cc_ref, sc_ref, w_ref, o_ref):
    n = x_ref.shape[0]
    x = x_ref[...]
    h = _norm_mod(x, g1_ref[...], mod_ref[SH1:SH1 + 1], mod_ref[SC1:SC1 + 1])
    xs = _dot_hi(m_ref[...], h)
    fmix = _channel_dft(xs[:n], xs[n:], cc_ref, sc_ref, _dot_hi).astype(BF16)
    o_ref[...] = x + mod_ref[G1:G1 + 1] * _dot(fmix, w_ref[...])


def _cos_sin(num, den, scale):
    ang = (2.0 * np.pi / den) * (num % den).astype(np.float64)
    return (np.cos(ang) * scale).astype(np.float32), (np.sin(ang) * scale).astype(np.float32)


def _dft_tables(n):
    gd = FNO_GROUP_DIM
    a = np.arange(gd, dtype=np.int64)
    cc, sc = _cos_sin(a[:, None] * a[None, :], gd, gd ** -0.5)
    if n % (FFT_N2 * FFT_B2) == 0 and n > 2 * FFT_N2:
        n1 = n // FFT_N2
        j1 = np.arange(n1, dtype=np.int64)
        c1, s1 = _cos_sin(j1[:, None] * j1[None, :], n1, n1 ** -0.5)
        eye = np.eye(FFT_B2, dtype=np.float32)
        k = j1[:, None, None] + n1 * np.arange(FFT_N2, dtype=np.int64)[None, :, None]
        j2 = np.arange(FFT_N2, dtype=np.int64)[None, None, :]
        mc, ms = _cos_sin(k * j2, n, FFT_N2 ** -0.5)
        m2 = np.concatenate([np.concatenate([mc, ms], axis=2),
                             np.concatenate([-ms, mc], axis=2)], axis=1)
        rows = np.arange(BF16_ROWS * FFT_B2)
        perm = np.zeros((BF16_ROWS * FFT_B2,) * 2, np.float32)
        perm[rows, (rows % FFT_B2) * BF16_ROWS + rows // FFT_B2] = 1.0
        tabs = {"kc": np.kron(c1, eye), "ks": np.kron(-s1, eye), "m2": m2, "perm": perm,
                "cc": cc, "sc": sc}
        return {k_: jnp.asarray(v).astype(BF16) for k_, v in tabs.items()}
    j = np.arange(n, dtype=np.int64)
    c, s = _cos_sin(j[:, None] * j[None, :], n, n ** -0.5)
    tabs = {"m": np.concatenate([c, -s], axis=0), "cc": cc, "sc": sc}
    return {k_: jnp.asarray(v) for k_, v in tabs.items()}


def _fourier_long(x, mod, g1, layer, w, tabs, tn):
    n, d = x.shape
    n1 = n // FFT_N2
    nb = n1 * FFT_B2
    yr, yi = pl.pallas_call(
        _fft_stage1_kernel,
        grid=(FFT_N2 // FFT_B1,),
        in_specs=[
            pl.BlockSpec((n1, FFT_B1, d), lambda s: (0, s, 0)),
            _const_spec(mod.shape), _const_spec(g1.shape),
            _const_spec((nb, nb)), _const_spec((nb, nb)),
        ],
        out_specs=[pl.BlockSpec((n1, FFT_B1, d), lambda s: (0, s, 0))] * 2,
        out_shape=[jax.ShapeDtypeStruct((n1, FFT_N2, d), BF16)] * 2,
        compiler_params=_params("arbitrary"),
        name="fft_stage1",
    )(x.reshape(n1, FFT_N2, d), mod, g1, tabs["kc"], tabs["ks"])
    rows = FFT_N2 * FFT_B2
    out = pl.pallas_call(
        _fft_stage2_kernel,
        grid=(n1 // FFT_B2, d // tn),
        in_specs=[
            pl.BlockSpec((rows, d), lambda a, j: (a, 0)),
            pl.BlockSpec((rows, d), lambda a, j: (a, 0)),
            pl.BlockSpec((FFT_B2, 2 * FFT_N2, 2 * FFT_N2), lambda a, j: (a, 0, 0)),
            _const_spec(tabs["cc"].shape), _const_spec(tabs["sc"].shape),
            _const_spec(tabs["perm"].shape),
            pl.BlockSpec((None, d, tn), lambda a, j: (layer, 0, j)),
            pl.BlockSpec((FFT_N2, FFT_B2, tn), lambda a, j: (0, a, j)),
            pl.BlockSpec((N_MOD, tn), lambda a, j: (0, j)),
        ],
        out_specs=pl.BlockSpec((FFT_N2, FFT_B2, tn), lambda a, j: (0, a, j)),
        out_shape=jax.ShapeDtypeStruct((FFT_N2, n1, d), F32),
        scratch_shapes=[pltpu.VMEM((rows, d), BF16), pltpu.VMEM((rows, d), BF16)],
        compiler_params=_params("arbitrary", "arbitrary"),
        name="fft_stage2",
    )(yr.reshape(n, d), yi.reshape(n, d), tabs["m2"], tabs["cc"], tabs["sc"], tabs["perm"], w,
      x.reshape(FFT_N2, n1, d), mod)
    return out.reshape(n, d)


def _fourier_short(x, mod, g1, layer, w, tabs):
    n, d = x.shape
    return pl.pallas_call(
        _dft_small_kernel,
        grid=(1,),
        in_specs=[_const_spec(a.shape) for a in (x, mod, g1, tabs["m"], tabs["cc"], tabs["sc"])]
        + [pl.BlockSpec((None, d, d), lambda i: (layer, 0, 0))],
        out_specs=pl.BlockSpec((n, d), lambda i: (0, 0)),
        out_shape=jax.ShapeDtypeStruct((n, d), F32),
        compiler_params=_params("arbitrary"),
        name="dft_ctx",
    )(x, mod, g1, tabs["m"], tabs["cc"], tabs["sc"], w)


def _fourier(x, mod, g1, layer, w):
    tabs = _dft_tables(x.shape[0])
    if "m2" in tabs:
        return _fourier_long(x, mod, g1, layer, w, tabs, 512)
    return _fourier_short(x, mod, g1, layer, w, tabs)


def _rot_cols(w):
    q = QK_ROPE // 4
    a1, a2, b1, b2 = w[..., :q], w[..., q:2 * q], w[..., 2 * q:3 * q], w[..., 3 * q:]
    return jnp.concatenate([-a2, a1, -b2, b1], axis=-1)


def _pad_lanes(w):
    return jnp.pad(w, [(0, 0)] * (w.ndim - 1) + [(0, LANES - w.shape[-1])])


def _mla_weights(w_dqkv, g_q, g_kv, w_uq, w_ukv):
    hd = N_HEADS * LANES
    kr = w_dqkv[:, Q_LORA + KV_LORA:]
    wd = jnp.concatenate([w_dqkv[:, :Q_LORA + KV_LORA], _pad_lanes(kr), _pad_lanes(_rot_cols(kr))],
                         axis=1).astype(BF16)
    q3 = w_uq.reshape(Q_LORA, N_HEADS, QK_NOPE + QK_ROPE)
    qr = q3[:, :, QK_NOPE:]
    wq = jnp.concatenate([q3[:, :, :QK_NOPE].reshape(Q_LORA, hd),
                          qr.reshape(Q_LORA, N_HEADS * QK_ROPE),
                          _rot_cols(qr).reshape(Q_LORA, N_HEADS * QK_ROPE)], axis=1).astype(BF16)
    kv3 = w_ukv.reshape(KV_LORA, N_HEADS, QK_NOPE + V_DIM)
    wk = kv3[:, :, :QK_NOPE].reshape(KV_LORA, hd).astype(BF16)
    wvT = kv3[:, :, QK_NOPE:].reshape(KV_LORA, N_HEADS * V_DIM).T.astype(BF16)
    return wd, g_q.reshape(1, -1), g_kv.reshape(1, -1), wq, wk, wvT


def _rope_tables(n):
    rows = n // GRID_W
    r, col = jnp.meshgrid(jnp.arange(rows, dtype=F32), jnp.arange(GRID_W, dtype=F32), indexing="ij")
    half = QK_ROPE // 2
    inv_freq = jnp.power(ROPE_BASE, -jnp.arange(0, half, 2, dtype=F32) / half)
    ang_r = r.reshape(-1)[:, None] * inv_freq
    ang_c = col.reshape(-1)[:, None] * inv_freq
    ang = jnp.concatenate([ang_r, ang_r, ang_c, ang_c], axis=-1)
    return jnp.tile(jnp.cos(ang), (1, 2)), jnp.tile(jnp.sin(ang), (1, 2))


def _row_tile(n, pref):
    return pref if n % pref == 0 else n


def kernel(x, c, ctx, c_ctx, ada_w, ada_b, norm1_g, norm2_g, mla_w_dqkv, mla_q_norm_g, mla_kv_norm_g,
           mla_w_uq, mla_w_ukv, mla_w_o, fno_w, ffn_w_up, ffn_conv_w, ffn_conv_b, ffn_w_down,
           final_norm_g):
    assert x.shape[0] == 1 and c.shape[0] == 1 and ctx.shape[0] == 1
    xs = x[0]
    cs = ctx[0]
    n, nc = xs.shape[0], cs.shape[0]
    mods = _ada_all(jnp.concatenate([c, c_ctx[None, :]], axis=0), ada_w, ada_b)
    cos, sin = _rope_tables(n)
    ones_c = jnp.ones((nc, LANES), F32)
    zeros_c = jnp.zeros((nc, LANES), F32)
    fin_g = final_norm_g.reshape(1, -1)
    w_up = ffn_w_up.astype(BF16)
    w_down = ffn_w_down.astype(BF16)
    w_fno = fno_w.astype(BF16)
    conv_b = ffn_conv_b.reshape(DEPTH, 1, -1)

    for i in range(DEPTH):
        kind = i % N_MIXERS
        j = i // N_MIXERS
        ctx_later = any(l % N_MIXERS == MIXER_MLA for l in range(i + 1, DEPTH))
        mod_x, mod_c = mods[i, 0], mods[i, 1]
        g1 = norm1_g[i].reshape(1, -1)
        g2 = norm2_g[i].reshape(1, -1)
        if kind == MIXER_MLA:
            wts = _mla_weights(mla_w_dqkv[j], mla_q_norm_g[j], mla_kv_norm_g[j], mla_w_uq[j], mla_w_ukv[j])
            w_o = mla_w_o[j].astype(BF16)
            qx, kx, vTx = _mla_proj(xs, mod_x, g1, wts, cos, sin, _row_tile(n, 256))
            qc, kc, vTc = _mla_proj(cs, mod_c, g1, wts, ones_c, zeros_c, _row_tile(nc, 256))
            ox = _flash(qx, [(kc, vTc), (kx, vTx)], _row_tile(n, 2048), 512)
            xs = _outproj(ox, w_o, xs, mod_x, _row_tile(n, 512))
            if ctx_later:
                oc = _flash(qc, [(kc, vTc)], _row_tile(nc, 256), 512)
                cs = _outproj(oc, w_o, cs, mod_c, _row_tile(nc, 256))
        else:
            xs = _fourier(xs, mod_x, g1, j, w_fno)
            if ctx_later:
                cs = _fourier(cs, mod_c, g1, j, w_fno)
        xs = _ffn(xs, mod_x, g2, i, w_up, ffn_conv_w, conv_b, w_down, fin_g, _row_tile(n, 512), 512,
                  final=(i == DEPTH - 1))
        if ctx_later:
            cs = _ffn(cs, mod_c, g2, i, w_up, ffn_conv_w, conv_b, w_down, fin_g, _row_tile(nc, 256), 512,
                      final=False)
    return xs[None]
```

```python
import functools
import math

import jax
import jax.numpy as jnp
import numpy as np
from jax import lax
from jax.experimental import pallas as pl
from jax.experimental.pallas import tpu as pltpu

D_MODEL = 2048
DEPTH = 4
GRID_W = 64
N_MIXERS = 2
MIXER_MLA = 0
N_HEADS = 16
Q_LORA = 512
KV_LORA = 512
QK_NOPE = 128
QK_ROPE = 64
V_DIM = 128
ROPE_BASE = 10000.0
SM_SCALE = (QK_NOPE + QK_ROPE) ** -0.5
FNO_GROUPS = 8
FNO_GROUP_DIM = D_MODEL // FNO_GROUPS
D_FF = 5632
NORM_EPS = 1e-6
N_MOD = 6

LANES = 128
SUBLANES = 8
BF16_ROWS = 16
FLASH_RING = 4
FLASH_AHEAD = 2
NORM_ROWS = BF16_ROWS
NORM_UNROLL = 8
FFN_COLS = 256
FFN_ROWS = 256
VMEM_LIMIT_BYTES = 56 * 1024 * 1024

HEAD_PAD = 2 * LANES
assert QK_NOPE == LANES and 2 * QK_ROPE == LANES and N_HEADS % 2 == 0
VT_ROWS = V_DIM + SUBLANES
Q_SCALE = SM_SCALE * math.log2(math.e)
FFT_N2 = 128
FFT_B2 = SUBLANES
FFT_B1 = BF16_ROWS

BF16 = jnp.bfloat16
F32 = jnp.float32
HIGHEST = lax.Precision.HIGHEST

SH1, SC1, G1, SH2, SC2, G2 = range(6)


def _params(*sem):
    return pltpu.CompilerParams(dimension_semantics=sem, vmem_limit_bytes=VMEM_LIMIT_BYTES)


def _const_spec(shape):
    nd = len(shape)
    return pl.BlockSpec(shape, lambda *_: (0,) * nd, pipeline_mode=pl.Buffered(1))


def _dot(a, b):
    return jnp.dot(a, b, preferred_element_type=F32)


def _dot_hi(a, b):
    return jnp.dot(a, b, preferred_element_type=F32, precision=HIGHEST)


def _rms(x, g):
    return x * lax.rsqrt(jnp.mean(x * x, axis=-1, keepdims=True) + NORM_EPS) * g


def _norm_mod(x, g, shift, scale):
    return _rms(x, g) * (1.0 + scale) + shift


def _ada_kernel(condT_ref, w_ref, b_ref, o_ref, acc_ref):
    k = pl.program_id(2)
    tk, tn = w_ref.shape[1], w_ref.shape[2]
    n_rows = acc_ref.shape[0]

    @pl.when(k == 0)
    def _():
        acc_ref[...] = jnp.zeros_like(acc_ref)

    w = w_ref[0]
    ct = condT_ref[pl.ds(pl.multiple_of(k * tk, tk), tk), :]
    s = ct * (1.0 / (1.0 + jnp.exp(-ct)))
    for r in range(n_rows):
        prod = w * s[:, r:r + 1]
        acc_ref[r] += prod.reshape(tk // SUBLANES, SUBLANES, tn).sum(axis=0)

    @pl.when(k == pl.num_programs(2) - 1)
    def _():
        for r in range(n_rows):
            o_ref[0, r:r + 1, :] = acc_ref[r].sum(axis=0, keepdims=True) + b_ref[0]


def _ada_all(cond, ada_w, ada_b):
    n_rows, d = cond.shape
    depth, _, nout = ada_w.shape
    tk, tn = 1024, 2048
    out = pl.pallas_call(
        _ada_kernel,
        grid=(depth, nout // tn, d // tk),
        in_specs=[
            pl.BlockSpec((d, n_rows), lambda l, j, k: (0, 0)),
            pl.BlockSpec((1, tk, tn), lambda l, j, k: (l, k, j)),
            pl.BlockSpec((1, 1, tn), lambda l, j, k: (l, 0, j)),
        ],
        out_specs=pl.BlockSpec((1, n_rows, tn), lambda l, j, k: (l, 0, j)),
        out_shape=jax.ShapeDtypeStruct((depth, n_rows, nout), F32),
        scratch_shapes=[pltpu.VMEM((n_rows, SUBLANES, tn), F32)],
        compiler_params=_params("arbitrary", "arbitrary", "arbitrary"),
        name="ada_mod",
    )(cond.T, ada_w, ada_b.reshape(depth, 1, nout))
    return out.reshape(depth, n_rows, N_MOD, d)


def _mla_proj_kernel(x_ref, mod_ref, g1_ref, wd_ref, gq_ref, gkv_ref, wq_ref, wk_ref, wvT_ref,
                     cos_ref, sin_ref, q_ref, k_ref, vT_ref):
    hd = N_HEADS * LANES
    h = _norm_mod(x_ref[...], g1_ref[...], mod_ref[SH1:SH1 + 1], mod_ref[SC1:SC1 + 1])
    d = _dot(h.astype(BF16), wd_ref[...])
    cq = _rms(d[:, :Q_LORA], gq_ref[...]).astype(BF16)
    ckv = _rms(d[:, Q_LORA:Q_LORA + KV_LORA], gkv_ref[...]).astype(BF16)
    cos = cos_ref[...]
    sin = sin_ref[...]
    base = Q_LORA + KV_LORA
    kr = (d[:, base:base + LANES] * cos + d[:, base + LANES:base + 2 * LANES] * sin).astype(BF16)
    hr = N_HEADS * QK_ROPE
    qall = _dot(cq, wq_ref[...])
    kn = _dot(ckv, wk_ref[...])
    vT = lax.dot_general(wvT_ref[...], ckv, (((1,), (1,)), ((), ())),
                         preferred_element_type=F32)
    tm = x_ref.shape[0]
    ones_row = (lax.broadcasted_iota(jnp.int32, (VT_ROWS - V_DIM, tm), 0) == 0).astype(BF16)
    low_half = lax.broadcasted_iota(jnp.int32, (tm, LANES), 1) < QK_ROPE
    for hh in range(N_HEADS):
        lo, hi = hh * LANES, (hh + 1) * LANES
        if hh % 2 == 0:
            po = hd + (hh // 2) * LANES
            pair = (qall[:, po:po + LANES] * cos + qall[:, hr + po:hr + po + LANES] * sin) * Q_SCALE
        own = pair if hh % 2 == 0 else pltpu.roll(pair, QK_ROPE, 1)
        qr = jnp.where(low_half, own, 0.0)
        q_ref[hh, :, 0:LANES] = (qall[:, lo:hi] * Q_SCALE).astype(BF16)
        q_ref[hh, :, LANES:HEAD_PAD] = qr.astype(BF16)
        k_ref[hh, :, 0:LANES] = kn[:, lo:hi].astype(BF16)
        k_ref[hh, :, LANES:HEAD_PAD] = kr
        vT_ref[hh, 0:V_DIM, :] = vT[lo:hi, :].astype(BF16)
        vT_ref[hh, V_DIM:VT_ROWS, :] = ones_row


def _mla_proj(x, mod, g1, wts, cos, sin, tm):
    n, d = x.shape
    wd, gq, gkv, wq, wk, wvT = wts
    return pl.pallas_call(
        _mla_proj_kernel,
        grid=(n // tm,),
        in_specs=[
            pl.BlockSpec((tm, d), lambda i: (i, 0)),
            _const_spec(mod.shape), _const_spec(g1.shape), _const_spec(wd.shape),
            _const_spec(gq.shape), _const_spec(gkv.shape), _const_spec(wq.shape),
            _const_spec(wk.shape), _const_spec(wvT.shape),
            pl.BlockSpec((tm, LANES), lambda i: (i, 0)),
            pl.BlockSpec((tm, LANES), lambda i: (i, 0)),
        ],
        out_specs=[
            pl.BlockSpec((N_HEADS, tm, HEAD_PAD), lambda i: (0, i, 0)),
            pl.BlockSpec((N_HEADS, tm, HEAD_PAD), lambda i: (0, i, 0)),
            pl.BlockSpec((N_HEADS, VT_ROWS, tm), lambda i: (0, 0, i)),
        ],
        out_shape=[
            jax.ShapeDtypeStruct((N_HEADS, n, HEAD_PAD), BF16),
            jax.ShapeDtypeStruct((N_HEADS, n, HEAD_PAD), BF16),
            jax.ShapeDtypeStruct((N_HEADS, VT_ROWS, n), BF16),
        ],
        compiler_params=_params("arbitrary"),
        name="mla_proj",
    )(x, mod, g1, wd, gq, gkv, wq, wk, wvT, cos, sin)


def _flash_kernel(*refs, seg_lens, tk):
    nseg = len(seg_lens)
    q_ref = refs[0]
    kv_refs = refs[1:1 + 2 * nseg]
    o_ref = refs[1 + 2 * nseg]
    m_ref, acc_ref, s_ref = refs[2 + 2 * nseg:]
    ring = s_ref.shape[0]
    q = q_ref[0]
    m_ref[...] = jnp.full_like(m_ref, -jnp.inf)
    acc_ref[...] = jnp.zeros_like(acc_ref)

    def chunk(c, ck):
        return pl.ds(c * ck if isinstance(c, int) else pl.multiple_of(c * ck, ck), ck)

    def scores(k_ref, c, ck):
        k = k_ref[0, chunk(c, ck), :]
        return lax.dot_general(k, q, (((1,), (1,)), ((), ())), preferred_element_type=F32)

    def update(s, vT_ref, c, ck):
        m_old = m_ref[...]
        m_new = jnp.maximum(m_old, jnp.max(s, axis=0, keepdims=True))
        p = jnp.exp2(s - m_new).astype(BF16)
        vT = vT_ref[0, :, chunk(c, ck)]
        acc_ref[...] = jnp.exp2(m_old - m_new) * acc_ref[...] + _dot(vT, p)
        m_ref[...] = m_new

    segs = []
    for si in range(nseg):
        ck = min(tk, seg_lens[si])
        segs.append((kv_refs[2 * si], kv_refs[2 * si + 1], ck, seg_lens[si] // ck))
    is_piped = lambda s: s[3] % ring == 0 and s[3] >= 2 * ring
    piped = [s for s in segs if is_piped(s)]
    assert len(piped) <= 1 and FLASH_AHEAD < ring
    short = [(seg, c) for seg in segs if not is_piped(seg) for c in range(seg[3])]
    s_short = [scores(seg[0], c, seg[2]) for seg, c in short]
    if piped:
        assert piped[0][2] == s_ref.shape[1]
        for a in range(FLASH_AHEAD):
            s_ref[a] = scores(piped[0][0], a, piped[0][2])
    for (seg, c), s in zip(short, s_short):
        update(s, seg[1], c, seg[2])
    for k_ref, vT_ref, ck, nch in piped:
        def step(c, u, prefetch):
            if prefetch:
                s_ref[(u + FLASH_AHEAD) % ring] = scores(k_ref, c + FLASH_AHEAD, ck)
            update(s_ref[u], vT_ref, c, ck)

        def body(t, carry):
            for u in range(ring):
                step(t * ring + u, u, True)
            return carry

        lax.fori_loop(0, nch // ring - 1, body, 0)
        for u in range(ring):
            c = nch - ring + u
            step(c, u, c + FLASH_AHEAD < nch)

    acc = acc_ref[...]
    o = acc[:V_DIM] * (1.0 / acc[V_DIM:V_DIM + 1])
    o_ref[...] = o.T.astype(o_ref.dtype)


def _flash(q, kvs, tq, tk):
    nh, n, _ = q.shape
    seg_lens = tuple(k.shape[1] for k, _ in kvs)
    in_specs = [pl.BlockSpec((1, tq, HEAD_PAD), lambda h, i: (h, i, 0))]
    args = [q]
    for k, vT in kvs:
        nk = k.shape[1]
        in_specs.append(pl.BlockSpec((1, nk, HEAD_PAD), lambda h, i: (h, 0, 0)))
        in_specs.append(pl.BlockSpec((1, VT_ROWS, nk), lambda h, i: (h, 0, 0)))
        args += [k, vT]
    ck = min(tk, max(seg_lens))
    return pl.pallas_call(
        functools.partial(_flash_kernel, seg_lens=seg_lens, tk=tk),
        grid=(nh, n // tq),
        in_specs=in_specs,
        out_specs=pl.BlockSpec((tq, V_DIM), lambda h, i: (i, h)),
        out_shape=jax.ShapeDtypeStruct((n, nh * V_DIM), BF16),
        scratch_shapes=[pltpu.VMEM((1, tq), F32), pltpu.VMEM((VT_ROWS, tq), F32),
                        pltpu.VMEM((FLASH_RING, ck, tq), F32)],
        compiler_params=_params("arbitrary", "arbitrary"),
        name="mla_flash",
    )(*args)


def _outproj_kernel(o_ref, w_ref, x_ref, mod_ref, out_ref):
    out_ref[...] = x_ref[...] + mod_ref[G1:G1 + 1] * _dot(o_ref[...], w_ref[...])


def _outproj(o, w, x, mod, tm):
    n, d = x.shape
    return pl.pallas_call(
        _outproj_kernel,
        grid=(n // tm,),
        in_specs=[
            pl.BlockSpec((tm, o.shape[1]), lambda i: (i, 0)),
            _const_spec(w.shape),
            pl.BlockSpec((tm, d), lambda i: (i, 0)),
            _const_spec(mod.shape),
        ],
        out_specs=pl.BlockSpec((tm, d), lambda i: (i, 0)),
        out_shape=jax.ShapeDtypeStruct((n, d), F32),
        compiler_params=_params("arbitrary"),
        name="mla_outproj",
    )(o, w, x, mod)


def _ffn_kernel(xp_ref, x_ref, xn_ref, mod_ref, g2_ref, wuv_ref, wug_ref, cwv_ref, cwg_ref,
                cbv_ref, cbg_ref, wd_ref, fg_ref, o_ref, h_ref, u_ref, *, tm, final):
    i = pl.program_id(0)
    f = pl.program_id(1)
    halo = BF16_ROWS
    rows = tm + 2 * halo

    @pl.when(f == 0)
    def _():
        g, sh, sc = g2_ref[...], mod_ref[SH2:SH2 + 1], mod_ref[SC2:SC2 + 1]
        hp = jnp.where(i > 0, _norm_mod(xp_ref[...], g, sh, sc), 0.0)
        hn = jnp.where(i < pl.num_programs(0) - 1, _norm_mod(xn_ref[...], g, sh, sc), 0.0)
        h_ref[0:halo] = hp.astype(BF16)
        h_ref[halo + tm:rows] = hn.astype(BF16)

        def norm_rows(r, carry):
            r0 = pl.multiple_of(r * NORM_ROWS, NORM_ROWS)
            xr = x_ref[pl.ds(r0, NORM_ROWS), :]
            h_ref[pl.ds(halo + r0, NORM_ROWS), :] = _norm_mod(xr, g, sh, sc).astype(BF16)
            return carry

        lax.fori_loop(0, tm // NORM_ROWS, norm_rows, 0, unroll=8)
        o_ref[...] = jnp.zeros_like(o_ref)

    tf = wd_ref.shape[0]
    tc = min(tf, FFN_COLS)
    tb = min(tm, FFN_ROWS)

    def conv(slot, cw_ref, cb_ref, cols):
        cw = cw_ref[:, cols]
        return (u_ref[slot, halo - 1:halo - 1 + tb, :] * cw[0:1]
                + u_ref[slot, halo:halo + tb, :] * cw[1:2]
                + u_ref[slot, halo + 1:halo + 1 + tb, :] * cw[2:3] + cb_ref[:, cols])

    pieces = [(b0, slice(c0, c0 + tc)) for b0 in range(0, tm, tb) for c0 in range(0, tf, tc)]
    for pi, (b0, cols) in enumerate(pieces):
        hb = h_ref[b0:b0 + tb + 2 * halo, :]
        u_ref[2 * pi] = _dot(hb, wuv_ref[:, cols])
        u_ref[2 * pi + 1] = _dot(hb, wug_ref[:, cols])
    for pi, (b0, cols) in enumerate(pieces):
        val = conv(2 * pi, cwv_ref, cbv_ref, cols)
        gate = conv(2 * pi + 1, cwg_ref, cbg_ref, cols)
        act = (val * (gate * (1.0 / (1.0 + jnp.exp(-gate))))).astype(BF16)
        o_ref[b0:b0 + tb, :] += _dot(act, wd_ref[cols, :])

    @pl.when(f == pl.num_programs(1) - 1)
    def _():
        gate2, fg = mod_ref[G2:G2 + 1], fg_ref[...]

        def finish_rows(r, carry):
            base = pl.multiple_of(r * (NORM_UNROLL * NORM_ROWS), NORM_UNROLL * NORM_ROWS)
            groups = [pl.ds(base + u * NORM_ROWS, NORM_ROWS) for u in range(NORM_UNROLL)]
            resid = lambda rr: x_ref[rr, :] + gate2 * o_ref[rr, :]
            if final:
                inv = [lax.rsqrt(jnp.mean(jnp.square(resid(rr)), axis=-1, keepdims=True) + NORM_EPS)
                       for rr in groups]
                for rr, r_inv in zip(groups, inv):
                    o_ref[rr, :] = resid(rr) * r_inv * fg
            else:
                for rr in groups:
                    o_ref[rr, :] = resid(rr)
            return carry

        lax.fori_loop(0, tm // (NORM_UNROLL * NORM_ROWS), finish_rows, 0)


def _ffn(x, mod, g2, layer, w_up, conv_w, conv_b, w_down, final_g, tm, tf, final):
    n, d = x.shape
    dff = w_down.shape[1]
    nf = dff // tf
    halo = BF16_ROWS
    hb = tm // halo
    last = n // halo - 1
    tb, tc = min(tm, FFN_ROWS), min(tf, FFN_COLS)
    return pl.pallas_call(
        functools.partial(_ffn_kernel, tm=tm, final=final),
        grid=(n // tm, nf),
        in_specs=[
            pl.BlockSpec((halo, d), lambda i, f: (jnp.maximum(i * hb - 1, 0), 0)),
            pl.BlockSpec((tm, d), lambda i, f: (i, 0)),
            pl.BlockSpec((halo, d), lambda i, f: (jnp.minimum((i + 1) * hb, last), 0)),
            _const_spec(mod.shape), _const_spec(g2.shape),
            pl.BlockSpec((None, d, tf), lambda i, f: (layer, 0, f)),
            pl.BlockSpec((None, d, tf), lambda i, f: (layer, 0, nf + f)),
            pl.BlockSpec((None, 3, tf), lambda i, f: (layer, 0, f)),
            pl.BlockSpec((None, 3, tf), lambda i, f: (layer, 0, nf + f)),
            pl.BlockSpec((None, 1, tf), lambda i, f: (layer, 0, f)),
            pl.BlockSpec((None, 1, tf), lambda i, f: (layer, 0, nf + f)),
            pl.BlockSpec((None, tf, d), lambda i, f: (layer, f, 0)),
            _const_spec(final_g.shape),
        ],
        out_specs=pl.BlockSpec((tm, d), lambda i, f: (i, 0)),
        out_shape=jax.ShapeDtypeStruct((n, d), F32),
        scratch_shapes=[pltpu.VMEM((tm + 2 * halo, d), BF16),
                        pltpu.VMEM((2 * (tm // tb) * (tf // tc), tb + 2 * halo, tc), F32)],
        compiler_params=_params("arbitrary", "arbitrary"),
        name="conv_ffn",
    )(x, x, x, mod, g2, w_up, w_up, conv_w, conv_w, conv_b, conv_b, w_down, final_g)


def _channel_dft(xr, xi, cc_ref, sc_ref, dot):
    parts = []
    for g in range(FNO_GROUPS):
        lo, hi = g * FNO_GROUP_DIM, (g + 1) * FNO_GROUP_DIM
        parts.append(dot(xr[:, lo:hi], cc_ref[...]) + dot(xi[:, lo:hi], sc_ref[...]))
    return jnp.concatenate(parts, axis=1)


def _fft_stage1_kernel(x_ref, mod_ref, g1_ref, kc_ref, ks_ref, yr_ref, yi_ref):
    n1, b1, d = x_ref.shape
    yr, yi = [], []
    for j0 in range(0, b1, FFT_B2):
        x = x_ref[:, j0:j0 + FFT_B2, :].reshape(n1 * FFT_B2, d)
        h = _norm_mod(x, g1_ref[...], mod_ref[SH1:SH1 + 1], mod_ref[SC1:SC1 + 1]).astype(BF16)
        yr.append(_dot(kc_ref[...], h).reshape(n1, FFT_B2, d))
        yi.append(_dot(ks_ref[...], h).reshape(n1, FFT_B2, d))
    yr_ref[...] = jnp.concatenate(yr, axis=1).astype(BF16)
    yi_ref[...] = jnp.concatenate(yi, axis=1).astype(BF16)


def _fft_stage2_kernel(yr_ref, yi_ref, m_ref, cc_ref, sc_ref, p_ref, w_ref, x_ref, mod_ref, o_ref,
                       f_ref, fp_ref):
    b2 = m_ref.shape[0]
    tn = o_ref.shape[2]

    @pl.when(pl.program_id(1) == 0)
    def _():
        for q in range(b2):
            rows = slice(q * FFT_N2, (q + 1) * FFT_N2)
            ys = jnp.concatenate([yr_ref[rows], yi_ref[rows]], axis=0)
            xs = _dot(m_ref[q], ys).astype(BF16)
            f_ref[rows] = _channel_dft(xs[:FFT_N2], xs[FFT_N2:], cc_ref, sc_ref, _dot).astype(BF16)
        pr = p_ref.shape[0]
        kb = pr // b2
        for b in range(FFT_N2 // kb):
            g = jnp.concatenate([f_ref[q * FFT_N2 + b * kb:q * FFT_N2 + (b + 1) * kb, :]
                                 for q in range(b2)], axis=0)
            fp_ref[b * pr:(b + 1) * pr, :] = _dot(p_ref[...], g).astype(BF16)

    y = _dot(fp_ref[...], w_ref[...])
    o_ref[...] = x_ref[...] + mod_ref[G1:G1 + 1].reshape(1, 1, tn) * y.reshape(FFT_N2, b2, tn)


def _dft_small_kernel(x_ref, mod_ref, g1_ref, m_ref, cc_ref, sc_ref, w_ref, o_ref):
    n = x_ref.shape[0]
    x = x_ref[...]
    h = _norm_mod(x, g1_ref[...], mod_ref[SH1:SH1 + 1], mod_ref[SC1:SC1 + 1])
    xs = _dot_hi(m_ref[...], h)
    fmix = _channel_dft(xs[:n], xs[n:], cc_ref, sc_ref, _dot_hi).astype(BF16)
    o_ref[...] = x + mod_ref[G1:G1 + 1] * _dot(fmix, w_ref[...])


def _cos_sin(num, den, scale):
    ang = (2.0 * np.pi / den) * (num % den).astype(np.float64)
    return (np.cos(ang) * scale).astype(np.float32), (np.sin(ang) * scale).astype(np.float32)


def _dft_tables(n):
    gd = FNO_GROUP_DIM
    a = np.arange(gd, dtype=np.int64)
    cc, sc = _cos_sin(a[:, None] * a[None, :], gd, gd ** -0.5)
    if n % (FFT_N2 * FFT_B2) == 0 and n > 2 * FFT_N2:
        n1 = n // FFT_N2
        j1 = np.arange(n1, dtype=np.int64)
        c1, s1 = _cos_sin(j1[:, None] * j1[None, :], n1, n1 ** -0.5)
        eye = np.eye(FFT_B2, dtype=np.float32)
        k = j1[:, None, None] + n1 * np.arange(FFT_N2, dtype=np.int64)[None, :, None]
        j2 = np.arange(FFT_N2, dtype=np.int64)[None, None, :]
        mc, ms = _cos_sin(k * j2, n, FFT_N2 ** -0.5)
        m2 = np.concatenate([np.concatenate([mc, ms], axis=2),
                             np.concatenate([-ms, mc], axis=2)], axis=1)
        rows = np.arange(BF16_ROWS * FFT_B2)
        perm = np.zeros((BF16_ROWS * FFT_B2,) * 2, np.float32)
        perm[rows, (rows % FFT_B2) * BF16_ROWS + rows // FFT_B2] = 1.0
        tabs = {"kc": np.kron(c1, eye), "ks": np.kron(-s1, eye), "m2": m2, "perm": perm,
                "cc": cc, "sc": sc}
        return {k_: jnp.asarray(v).astype(BF16) for k_, v in tabs.items()}
    j = np.arange(n, dtype=np.int64)
    c, s = _cos_sin(j[:, None] * j[None, :], n, n ** -0.5)
    tabs = {"m": np.concatenate([c, -s], axis=0), "cc": cc, "sc": sc}
    return {k_: jnp.asarray(v) for k_, v in tabs.items()}


def _fourier_long(x, mod, g1, layer, w, tabs, tn):
    n, d = x.shape
    n1 = n // FFT_N2
    nb = n1 * FFT_B2
    yr, yi = pl.pallas_call(
        _fft_stage1_kernel,
        grid=(FFT_N2 // FFT_B1,),
        in_specs=[
            pl.BlockSpec((n1, FFT_B1, d), lambda s: (0, s, 0)),
            _const_spec(mod.shape), _const_spec(g1.shape),
            _const_spec((nb, nb)), _const_spec((nb, nb)),
        ],
        out_specs=[pl.BlockSpec((n1, FFT_B1, d), lambda s: (0, s, 0))] * 2,
        out_shape=[jax.ShapeDtypeStruct((n1, FFT_N2, d), BF16)] * 2,
        compiler_params=_params("arbitrary"),
        name="fft_stage1",
    )(x.reshape(n1, FFT_N2, d), mod, g1, tabs["kc"], tabs["ks"])
    rows = FFT_N2 * FFT_B2
    out = pl.pallas_call(
        _fft_stage2_kernel,
        grid=(n1 // FFT_B2, d // tn),
        in_specs=[
            pl.BlockSpec((rows, d), lambda a, j: (a, 0)),
            pl.BlockSpec((rows, d), lambda a, j: (a, 0)),
            pl.BlockSpec((FFT_B2, 2 * FFT_N2, 2 * FFT_N2), lambda a, j: (a, 0, 0)),
            _const_spec(tabs["cc"].shape), _const_spec(tabs["sc"].shape),
            _const_spec(tabs["perm"].shape),
            pl.BlockSpec((None, d, tn), lambda a, j: (layer, 0, j)),
            pl.BlockSpec((FFT_N2, FFT_B2, tn), lambda a, j: (0, a, j)),
            pl.BlockSpec((N_MOD, tn), lambda a, j: (0, j)),
        ],
        out_specs=pl.BlockSpec((FFT_N2, FFT_B2, tn), lambda a, j: (0, a, j)),
        out_shape=jax.ShapeDtypeStruct((FFT_N2, n1, d), F32),
        scratch_shapes=[pltpu.VMEM((rows, d), BF16), pltpu.VMEM((rows, d), BF16)],
        compiler_params=_params("arbitrary", "arbitrary"),
        name="fft_stage2",
    )(yr.reshape(n, d), yi.reshape(n, d), tabs["m2"], tabs["cc"], tabs["sc"], tabs["perm"], w,
      x.reshape(FFT_N2, n1, d), mod)
    return out.reshape(n, d)


def _fourier_short(x, mod, g1, layer, w, tabs):
    n, d = x.shape
    return pl.pallas_call(
        _dft_small_kernel,
        grid=(1,),
        in_specs=[_const_spec(a.shape) for a in (x, mod, g1, tabs["m"], tabs["cc"], tabs["sc"])]
        + [pl.BlockSpec((None, d, d), lambda i: (layer, 0, 0))],
        out_specs=pl.BlockSpec((n, d), lambda i: (0, 0)),
        out_shape=jax.ShapeDtypeStruct((n, d), F32),
        compiler_params=_params("arbitrary"),
        name="dft_ctx",
    )(x, mod, g1, tabs["m"], tabs["cc"], tabs["sc"], w)


def _fourier(x, mod, g1, layer, w):
    tabs = _dft_tables(x.shape[0])
    if "m2" in tabs:
        return _fourier_long(x, mod, g1, layer, w, tabs, 512)
    return _fourier_short(x, mod, g1, layer, w, tabs)


def _rot_cols(w):
    q = QK_ROPE // 4
    a1, a2, b1, b2 = w[..., :q], w[..., q:2 * q], w[..., 2 * q:3 * q], w[..., 3 * q:]
    return jnp.concatenate([-a2, a1, -b2, b1], axis=-1)


def _pad_lanes(w):
    return jnp.pad(w, [(0, 0)] * (w.ndim - 1) + [(0, LANES - w.shape[-1])])


def _mla_weights(w_dqkv, g_q, g_kv, w_uq, w_ukv):
    hd = N_HEADS * LANES
    kr = w_dqkv[:, Q_LORA + KV_LORA:]
    wd = jnp.concatenate([w_dqkv[:, :Q_LORA + KV_LORA], _pad_lanes(kr), _pad_lanes(_rot_cols(kr))],
                         axis=1).astype(BF16)
    q3 = w_uq.reshape(Q_LORA, N_HEADS, QK_NOPE + QK_ROPE)
    qr = q3[:, :, QK_NOPE:]
    wq = jnp.concatenate([q3[:, :, :QK_NOPE].reshape(Q_LORA, hd),
                          qr.reshape(Q_LORA, N_HEADS * QK_ROPE),
                          _rot_cols(qr).reshape(Q_LORA, N_HEADS * QK_ROPE)], axis=1).astype(BF16)
    kv3 = w_ukv.reshape(KV_LORA, N_HEADS, QK_NOPE + V_DIM)
    wk = kv3[:, :, :QK_NOPE].reshape(KV_LORA, hd).astype(BF16)
    wvT = kv3[:, :, QK_NOPE:].reshape(KV_LORA, N_HEADS * V_DIM).T.astype(BF16)
    return wd, g_q.reshape(1, -1), g_kv.reshape(1, -1), wq, wk, wvT


def _rope_tables(n):
    rows = n // GRID_W
    r, col = jnp.meshgrid(jnp.arange(rows, dtype=F32), jnp.arange(GRID_W, dtype=F32), indexing="ij")
    half = QK_ROPE // 2
    inv_freq = jnp.power(ROPE_BASE, -jnp.arange(0, half, 2, dtype=F32) / half)
    ang_r = r.reshape(-1)[:, None] * inv_freq
    ang_c = col.reshape(-1)[:, None] * inv_freq
    ang = jnp.concatenate([ang_r, ang_r, ang_c, ang_c], axis=-1)
    return jnp.tile(jnp.cos(ang), (1, 2)), jnp.tile(jnp.sin(ang), (1, 2))


def _row_tile(n, pref):
    return pref if n % pref == 0 else n


def kernel(x, c, ctx, c_ctx, ada_w, ada_b, norm1_g, norm2_g, mla_w_dqkv, mla_q_norm_g, mla_kv_norm_g,
           mla_w_uq, mla_w_ukv, mla_w_o, fno_w, ffn_w_up, ffn_conv_w, ffn_conv_b, ffn_w_down,
           final_norm_g):
    assert x.shape[0] == 1 and c.shape[0] == 1 and ctx.shape[0] == 1
    xs = x[0]
    cs = ctx[0]
    n, nc = xs.shape[0], cs.shape[0]
    mods = _ada_all(jnp.concatenate([c, c_ctx[None, :]], axis=0), ada_w, ada_b)
    cos, sin = _rope_tables(n)
    ones_c = jnp.ones((nc, LANES), F32)
    zeros_c = jnp.zeros((nc, LANES), F32)
    fin_g = final_norm_g.reshape(1, -1)
    w_up = ffn_w_up.astype(BF16)
    w_down = ffn_w_down.astype(BF16)
    w_fno = fno_w.astype(BF16)
    conv_b = ffn_conv_b.reshape(DEPTH, 1, -1)

    for i in range(DEPTH):
        kind = i % N_MIXERS
        j = i // N_MIXERS
        ctx_later = any(l % N_MIXERS == MIXER_MLA for l in range(i + 1, DEPTH))
        mod_x, mod_c = mods[i, 0], mods[i, 1]
        g1 = norm1_g[i].reshape(1, -1)
        g2 = norm2_g[i].reshape(1, -1)
        if kind == MIXER_MLA:
            wts = _mla_weights(mla_w_dqkv[j], mla_q_norm_g[j], mla_kv_norm_g[j], mla_w_uq[j], mla_w_ukv[j])
            w_o = mla_w_o[j].astype(BF16)
            qx, kx, vTx = _mla_proj(xs, mod_x, g1, wts, cos, sin, _row_tile(n, 256))
            qc, kc, vTc = _mla_proj(cs, mod_c, g1, wts, ones_c, zeros_c, _row_tile(nc, 256))
            ox = _flash(qx, [(kc, vTc), (kx, vTx)], _row_tile(n, 2048), 512)
            xs = _outproj(ox, w_o, xs, mod_x, _row_tile(n, 512))
            if ctx_later:
                oc = _flash(qc, [(kc, vTc)], _row_tile(nc, 256), 512)
                cs = _outproj(oc, w_o, cs, mod_c, _row_tile(nc, 256))
        else:
            xs = _fourier(xs, mod_x, g1, j, w_fno)
            if ctx_later:
                cs = _fourier(cs, mod_c, g1, j, w_fno)
        xs = _ffn(xs, mod_x, g2, i, w_up, ffn_conv_w, conv_b, w_down, fin_g, _row_tile(n, 512), 512,
                  final=(i == DEPTH - 1))
        if ctx_later:
            cs = _ffn(cs, mod_c, g2, i, w_up, ffn_conv_w, conv_b, w_down, fin_g, _row_tile(nc, 256), 512,
                      final=False)
    return xs[None]
```

```python
import functools
import math

import jax
import jax.numpy as jnp
import numpy as np
from jax import lax
from jax.experimental import pallas as pl
from jax.experimental.pallas import tpu as pltpu

D_MODEL = 2048
DEPTH = 4
GRID_W = 64
N_MIXERS = 2
MIXER_MLA = 0
N_HEADS = 16
Q_LORA = 512
KV_LORA = 512
QK_NOPE = 128
QK_ROPE = 64
V_DIM = 128
ROPE_BASE = 10000.0
SM_SCALE = (QK_NOPE + QK_ROPE) ** -0.5
FNO_GROUPS = 8
FNO_GROUP_DIM = D_MODEL // FNO_GROUPS
D_FF = 5632
NORM_EPS = 1e-6
N_MOD = 6

LANES = 128
SUBLANES = 8
BF16_ROWS = 16
FLASH_RING = 4
FLASH_AHEAD = 2
NORM_ROWS = BF16_ROWS
NORM_UNROLL = 8
FFN_COLS = 256
FFN_ROWS = 256
VMEM_LIMIT_BYTES = 56 * 1024 * 1024

HEAD_PAD = 2 * LANES
assert QK_NOPE == LANES and 2 * QK_ROPE == LANES and N_HEADS % 2 == 0
VT_ROWS = V_DIM + SUBLANES
Q_SCALE = SM_SCALE * math.log2(math.e)
FFT_N2 = 128
FFT_B2 = SUBLANES
FFT_B1 = BF16_ROWS

BF16 = jnp.bfloat16
F32 = jnp.float32
HIGHEST = lax.Precision.HIGHEST

SH1, SC1, G1, SH2, SC2, G2 = range(6)


def _params(*sem):
    return pltpu.CompilerParams(dimension_semantics=sem, vmem_limit_bytes=VMEM_LIMIT_BYTES)


def _const_spec(shape):
    nd = len(shape)
    return pl.BlockSpec(shape, lambda *_: (0,) * nd, pipeline_mode=pl.Buffered(1))


def _dot(a, b):
    return jnp.dot(a, b, preferred_element_type=F32)


def _dot_hi(a, b):
    return jnp.dot(a, b, preferred_element_type=F32, precision=HIGHEST)


def _rms(x, g):
    return x * lax.rsqrt(jnp.mean(x * x, axis=-1, keepdims=True) + NORM_EPS) * g


def _norm_mod(x, g, shift, scale):
    return _rms(x, g) * (1.0 + scale) + shift


def _ada_kernel(condT_ref, w_ref, b_ref, o_ref, acc_ref):
    k = pl.program_id(2)
    tk, tn = w_ref.shape[1], w_ref.shape[2]
    n_rows = acc_ref.shape[0]

    @pl.when(k == 0)
    def _():
        acc_ref[...] = jnp.zeros_like(acc_ref)

    w = w_ref[0]
    ct = condT_ref[pl.ds(pl.multiple_of(k * tk, tk), tk), :]
    s = ct * (1.0 / (1.0 + jnp.exp(-ct)))
    for r in range(n_rows):
        prod = w * s[:, r:r + 1]
        acc_ref[r] += prod.reshape(tk // SUBLANES, SUBLANES, tn).sum(axis=0)

    @pl.when(k == pl.num_programs(2) - 1)
    def _():
        for r in range(n_rows):
            o_ref[0, r:r + 1, :] = acc_ref[r].sum(axis=0, keepdims=True) + b_ref[0]


def _ada_all(cond, ada_w, ada_b):
    n_rows, d = cond.shape
    depth, _, nout = ada_w.shape
    tk, tn = 1024, 2048
    out = pl.pallas_call(
        _ada_kernel,
        grid=(depth, nout // tn, d // tk),
        in_specs=[
            pl.BlockSpec((d, n_rows), lambda l, j, k: (0, 0)),
            pl.BlockSpec((1, tk, tn), lambda l, j, k: (l, k, j)),
            pl.BlockSpec((1, 1, tn), lambda l, j, k: (l, 0, j)),
        ],
        out_specs=pl.BlockSpec((1, n_rows, tn), lambda l, j, k: (l, 0, j)),
        out_shape=jax.ShapeDtypeStruct((depth, n_rows, nout), F32),
        scratch_shapes=[pltpu.VMEM((n_rows, SUBLANES, tn), F32)],
        compiler_params=_params("arbitrary", "arbitrary", "arbitrary"),
        name="ada_mod",
    )(cond.T, ada_w, ada_b.reshape(depth, 1, nout))
    return out.reshape(depth, n_rows, N_MOD, d)


def _mla_proj_kernel(x_ref, mod_ref, g1_ref, wd_ref, gq_ref, gkv_ref, wq_ref, wk_ref, wvT_ref,
                     cos_ref, sin_ref, q_ref, k_ref, vT_ref):
    hd = N_HEADS * LANES
    h = _norm_mod(x_ref[...], g1_ref[...], mod_ref[SH1:SH1 + 1], mod_ref[SC1:SC1 + 1])
    d = _dot(h.astype(BF16), wd_ref[...])
    cq = _rms(d[:, :Q_LORA], gq_ref[...]).astype(BF16)
    ckv = _rms(d[:, Q_LORA:Q_LORA + KV_LORA], gkv_ref[...]).astype(BF16)
    cos = cos_ref[...]
    sin = sin_ref[...]
    base = Q_LORA + KV_LORA
    kr = (d[:, base:base + LANES] * cos + d[:, base + LANES:base + 2 * LANES] * sin).astype(BF16)
    hr = N_HEADS * QK_ROPE
    qall = _dot(cq, wq_ref[...])
    kn = _dot(ckv, wk_ref[...])
    vT = lax.dot_general(wvT_ref[...], ckv, (((1,), (1,)), ((), ())),
                         preferred_element_type=F32)
    tm = x_ref.shape[0]
    ones_row = (lax.broadcasted_iota(jnp.int32, (VT_ROWS - V_DIM, tm), 0) == 0).astype(BF16)
    low_half = lax.broadcasted_iota(jnp.int32, (tm, LANES), 1) < QK_ROPE
    for hh in range(N_HEADS):
        lo, hi = hh * LANES, (hh + 1) * LANES
        if hh % 2 == 0:
            po = hd + (hh // 2) * LANES
            pair = (qall[:, po:po + LANES] * cos + qall[:, hr + po:hr + po + LANES] * sin) * Q_SCALE
        own = pair if hh % 2 == 0 else pltpu.roll(pair, QK_ROPE, 1)
        qr = jnp.where(low_half, own, 0.0)
        q_ref[hh, :, 0:LANES] = (qall[:, lo:hi] * Q_SCALE).astype(BF16)
        q_ref[hh, :, LANES:HEAD_PAD] = qr.astype(BF16)
        k_ref[hh, :, 0:LANES] = kn[:, lo:hi].astype(BF16)
        k_ref[hh, :, LANES:HEAD_PAD] = kr
        vT_ref[hh, 0:V_DIM, :] = vT[lo:hi, :].astype(BF16)
        vT_ref[hh, V_DIM:VT_ROWS, :] = ones_row


def _mla_proj(x, mod, g1, wts, cos, sin, tm):
    n, d = x.shape
    wd, gq, gkv, wq, wk, wvT = wts
    return pl.pallas_call(
        _mla_proj_kernel,
        grid=(n // tm,),
        in_specs=[
            pl.BlockSpec((tm, d), lambda i: (i, 0)),
            _const_spec(mod.shape), _const_spec(g1.shape), _const_spec(wd.shape),
            _const_spec(gq.shape), _const_spec(gkv.shape), _const_spec(wq.shape),
            _const_spec(wk.shape), _const_spec(wvT.shape),
            pl.BlockSpec((tm, LANES), lambda i: (i, 0)),
            pl.BlockSpec((tm, LANES), lambda i: (i, 0)),
        ],
        out_specs=[
            pl.BlockSpec((N_HEADS, tm, HEAD_PAD), lambda i: (0, i, 0)),
            pl.BlockSpec((N_HEADS, tm, HEAD_PAD), lambda i: (0, i, 0)),
            pl.BlockSpec((N_HEADS, VT_ROWS, tm), lambda i: (0, 0, i)),
        ],
        out_shape=[
            jax.ShapeDtypeStruct((N_HEADS, n, HEAD_PAD), BF16),
            jax.ShapeDtypeStruct((N_HEADS, n, HEAD_PAD), BF16),
            jax.ShapeDtypeStruct((N_HEADS, VT_ROWS, n), BF16),
        ],
        compiler_params=_params("arbitrary"),
        name="mla_proj",
    )(x, mod, g1, wd, gq, gkv, wq, wk, wvT, cos, sin)


def _flash_kernel(*refs, seg_lens, tk, ncast):
    nseg = len(seg_lens)
    q_ref = refs[0]
    kv_refs = refs[1:1 + 2 * nseg]
    cast_in = refs[1 + 2 * nseg:1 + 2 * nseg + ncast]
    o_ref = refs[1 + 2 * nseg + ncast]
    cast_out = refs[2 + 2 * nseg + ncast:2 + 2 * nseg + 2 * ncast]
    m_ref, acc_ref, s_ref = refs[2 + 2 * nseg + 2 * ncast:]
    for src, dst in zip(cast_in, cast_out):
        dst[...] = src[...].astype(dst.dtype)
    ring = s_ref.shape[0]
    q = q_ref[0]
    m_ref[...] = jnp.full_like(m_ref, -jnp.inf)
    acc_ref[...] = jnp.zeros_like(acc_ref)

    def chunk(c, ck):
        return pl.ds(c * ck if isinstance(c, int) else pl.multiple_of(c * ck, ck), ck)

    def scores(k_ref, c, ck):
        k = k_ref[0, chunk(c, ck), :]
        return lax.dot_general(k, q, (((1,), (1,)), ((), ())), preferred_element_type=F32)

    def update(s, vT_ref, c, ck):
        m_old = m_ref[...]
        m_new = jnp.maximum(m_old, jnp.max(s, axis=0, keepdims=True))
        p = jnp.exp2(s - m_new).astype(BF16)
        vT = vT_ref[0, :, chunk(c, ck)]
        acc_ref[...] = jnp.exp2(m_old - m_new) * acc_ref[...] + _dot(vT, p)
        m_ref[...] = m_new

    segs = []
    for si in range(nseg):
        ck = min(tk, seg_lens[si])
        segs.append((kv_refs[2 * si], kv_refs[2 * si + 1], ck, seg_lens[si] // ck))
    is_piped = lambda s: s[3] % ring == 0 and s[3] >= 2 * ring
    piped = [s for s in segs if is_piped(s)]
    assert len(piped) <= 1 and FLASH_AHEAD < ring
    short = [(seg, c) for seg in segs if not is_piped(seg) for c in range(seg[3])]
    s_short = [scores(seg[0], c, seg[2]) for seg, c in short]
    if piped:
        assert piped[0][2] == s_ref.shape[1]
        for a in range(FLASH_AHEAD):
            s_ref[a] = scores(piped[0][0], a, piped[0][2])
    for (seg, c), s in zip(short, s_short):
        update(s, seg[1], c, seg[2])
    for k_ref, vT_ref, ck, nch in piped:
        def step(c, u, prefetch):
            if prefetch:
                s_ref[(u + FLASH_AHEAD) % ring] = scores(k_ref, c + FLASH_AHEAD, ck)
            update(s_ref[u], vT_ref, c, ck)

        def body(t, carry):
            for u in range(ring):
                step(t * ring + u, u, True)
            return carry

        lax.fori_loop(0, nch // ring - 1, body, 0)
        for u in range(ring):
            c = nch - ring + u
            step(c, u, c + FLASH_AHEAD < nch)

    acc = acc_ref[...]
    o = acc[:V_DIM] * (1.0 / acc[V_DIM:V_DIM + 1])
    o_ref[...] = o.T.astype(o_ref.dtype)


def _flash(q, kvs, tq, tk, casts=()):
    nh, n, _ = q.shape
    nq = n // tq
    seg_lens = tuple(k.shape[1] for k, _ in kvs)
    in_specs = [pl.BlockSpec((1, tq, HEAD_PAD), lambda h, i: (h, i, 0))]
    args = [q]
    for k, vT in kvs:
        nk = k.shape[1]
        in_specs.append(pl.BlockSpec((1, nk, HEAD_PAD), lambda h, i: (h, 0, 0)))
        in_specs.append(pl.BlockSpec((1, VT_ROWS, nk), lambda h, i: (h, 0, 0)))
        args += [k, vT]
    out_specs = [pl.BlockSpec((tq, V_DIM), lambda h, i: (i, h))]
    out_shape = [jax.ShapeDtypeStruct((n, nh * V_DIM), BF16)]
    for w, first, nrows in casts:
        slab = nrows // (nh * nq)
        assert slab * nh * nq == nrows and slab % BF16_ROWS == 0 and first % slab == 0
        in_specs.append(pl.BlockSpec((slab, w.shape[1]),
                                     lambda h, i, b=first // slab: (b + h * nq + i, 0)))
        out_specs.append(pl.BlockSpec((slab, w.shape[1]), lambda h, i: (h * nq + i, 0)))
        out_shape.append(jax.ShapeDtypeStruct((nrows, w.shape[1]), BF16))
        args.append(w)
    ck = min(tk, max(seg_lens))
    outs = pl.pallas_call(
        functools.partial(_flash_kernel, seg_lens=seg_lens, tk=tk, ncast=len(casts)),
        grid=(nh, nq),
        in_specs=in_specs,
        out_specs=out_specs,
        out_shape=out_shape,
        scratch_shapes=[pltpu.VMEM((1, tq), F32), pltpu.VMEM((VT_ROWS, tq), F32),
                        pltpu.VMEM((FLASH_RING, ck, tq), F32)],
        compiler_params=_params("arbitrary", "arbitrary"),
        name="mla_flash",
    )(*args)
    return outs[0], outs[1:]


def _outproj_kernel(o_ref, w_ref, x_ref, mod_ref, out_ref):
    out_ref[...] = x_ref[...] + mod_ref[G1:G1 + 1] * _dot(o_ref[...], w_ref[...])


def _outproj(o, w, x, mod, tm):
    n, d = x.shape
    return pl.pallas_call(
        _outproj_kernel,
        grid=(n // tm,),
        in_specs=[
            pl.BlockSpec((tm, o.shape[1]), lambda i: (i, 0)),
            _const_spec(w.shape),
            pl.BlockSpec((tm, d), lambda i: (i, 0)),
            _const_spec(mod.shape),
        ],
        out_specs=pl.BlockSpec((tm, d), lambda i: (i, 0)),
        out_shape=jax.ShapeDtypeStruct((n, d), F32),
        compiler_params=_params("arbitrary"),
        name="mla_outproj",
    )(o, w, x, mod)


def _ffn_kernel(xp_ref, x_ref, xn_ref, mod_ref, g2_ref, wuv_ref, wug_ref, cwv_ref, cwg_ref,
                cbv_ref, cbg_ref, wd_ref, fg_ref, o_ref, h_ref, u_ref, *, tm, final):
    i = pl.program_id(0)
    f = pl.program_id(1)
    halo = BF16_ROWS
    rows = tm + 2 * halo

    @pl.when(f == 0)
    def _():
        g, sh, sc = g2_ref[...], mod_ref[SH2:SH2 + 1], mod_ref[SC2:SC2 + 1]
        hp = jnp.where(i > 0, _norm_mod(xp_ref[...], g, sh, sc), 0.0)
        hn = jnp.where(i < pl.num_programs(0) - 1, _norm_mod(xn_ref[...], g, sh, sc), 0.0)
        h_ref[0:halo] = hp.astype(BF16)
        h_ref[halo + tm:rows] = hn.astype(BF16)

        def norm_rows(r, carry):
            r0 = pl.multiple_of(r * NORM_ROWS, NORM_ROWS)
            xr = x_ref[pl.ds(r0, NORM_ROWS), :]
            h_ref[pl.ds(halo + r0, NORM_ROWS), :] = _norm_mod(xr, g, sh, sc).astype(BF16)
            return carry

        lax.fori_loop(0, tm // NORM_ROWS, norm_rows, 0, unroll=8)
        o_ref[...] = jnp.zeros_like(o_ref)

    tf = wd_ref.shape[0]
    tc = min(tf, FFN_COLS)
    tb = min(tm, FFN_ROWS)

    def conv(slot, cw_ref, cb_ref, cols):
        cw = cw_ref[:, cols]
        return (u_ref[slot, halo - 1:halo - 1 + tb, :] * cw[0:1]
                + u_ref[slot, halo:halo + tb, :] * cw[1:2]
                + u_ref[slot, halo + 1:halo + 1 + tb, :] * cw[2:3] + cb_ref[:, cols])

    pieces = [(b0, slice(c0, c0 + tc)) for b0 in range(0, tm, tb) for c0 in range(0, tf, tc)]
    for pi, (b0, cols) in enumerate(pieces):
        hb = h_ref[b0:b0 + tb + 2 * halo, :]
        u_ref[2 * pi] = _dot(hb, wuv_ref[:, cols])
        u_ref[2 * pi + 1] = _dot(hb, wug_ref[:, cols])
    for pi, (b0, cols) in enumerate(pieces):
        val = conv(2 * pi, cwv_ref, cbv_ref, cols)
        gate = conv(2 * pi + 1, cwg_ref, cbg_ref, cols)
        act = (val * (gate * (1.0 / (1.0 + jnp.exp(-gate))))).astype(BF16)
        o_ref[b0:b0 + tb, :] += _dot(act, wd_ref[cols, :])

    @pl.when(f == pl.num_programs(1) - 1)
    def _():
        gate2, fg = mod_ref[G2:G2 + 1], fg_ref[...]

        def finish_rows(r, carry):
            base = pl.multiple_of(r * (NORM_UNROLL * NORM_ROWS), NORM_UNROLL * NORM_ROWS)
            groups = [pl.ds(base + u * NORM_ROWS, NORM_ROWS) for u in range(NORM_UNROLL)]
            resid = lambda rr: x_ref[rr, :] + gate2 * o_ref[rr, :]
            if final:
                inv = [lax.rsqrt(jnp.mean(jnp.square(resid(rr)), axis=-1, keepdims=True) + NORM_EPS)
                       for rr in groups]
                for rr, r_inv in zip(groups, inv):
                    o_ref[rr, :] = resid(rr) * r_inv * fg
            else:
                for rr in groups:
                    o_ref[rr, :] = resid(rr)
            return carry

        lax.fori_loop(0, tm // (NORM_UNROLL * NORM_ROWS), finish_rows, 0)


def _ffn(x, mod, g2, layer, wlayer, w_up, conv_w, conv_b, w_down, final_g, tm, tf, final):
    n, d = x.shape
    dff = w_down.shape[1]
    nf = dff // tf
    halo = BF16_ROWS
    hb = tm // halo
    last = n // halo - 1
    tb, tc = min(tm, FFN_ROWS), min(tf, FFN_COLS)
    return pl.pallas_call(
        functools.partial(_ffn_kernel, tm=tm, final=final),
        grid=(n // tm, nf),
        in_specs=[
            pl.BlockSpec((halo, d), lambda i, f: (jnp.maximum(i * hb - 1, 0), 0)),
            pl.BlockSpec((tm, d), lambda i, f: (i, 0)),
            pl.BlockSpec((halo, d), lambda i, f: (jnp.minimum((i + 1) * hb, last), 0)),
            _const_spec(mod.shape), _const_spec(g2.shape),
            pl.BlockSpec((None, d, tf), lambda i, f: (wlayer, 0, f)),
            pl.BlockSpec((None, d, tf), lambda i, f: (wlayer, 0, nf + f)),
            pl.BlockSpec((None, 3, tf), lambda i, f: (layer, 0, f)),
            pl.BlockSpec((None, 3, tf), lambda i, f: (layer, 0, nf + f)),
            pl.BlockSpec((None, 1, tf), lambda i, f: (layer, 0, f)),
            pl.BlockSpec((None, 1, tf), lambda i, f: (layer, 0, nf + f)),
            pl.BlockSpec((None, tf, d), lambda i, f: (wlayer, f, 0)),
            _const_spec(final_g.shape),
        ],
        out_specs=pl.BlockSpec((tm, d), lambda i, f: (i, 0)),
        out_shape=jax.ShapeDtypeStruct((n, d), F32),
        scratch_shapes=[pltpu.VMEM((tm + 2 * halo, d), BF16),
                        pltpu.VMEM((2 * (tm // tb) * (tf // tc), tb + 2 * halo, tc), F32)],
        compiler_params=_params("arbitrary", "arbitrary"),
        name="conv_ffn",
    )(x, x, x, mod, g2, w_up, w_up, conv_w, conv_w, conv_b, conv_b, w_down, final_g)


def _channel_dft(xr, xi, cc_ref, sc_ref, dot):
    parts = []
    for g in range(FNO_GROUPS):
        lo, hi = g * FNO_GROUP_DIM, (g + 1) * FNO_GROUP_DIM
        parts.append(dot(xr[:, lo:hi], cc_ref[...]) + dot(xi[:, lo:hi], sc_ref[...]))
    return jnp.concatenate(parts, axis=1)


def _fft_stage1_kernel(x_ref, mod_ref, g1_ref, kc_ref, ks_ref, yr_ref, yi_ref):
    n1, b1, d = x_ref.shape
    yr, yi = [], []
    for j0 in range(0, b1, FFT_B2):
        x = x_ref[:, j0:j0 + FFT_B2, :].reshape(n1 * FFT_B2, d)
        h = _norm_mod(x, g1_ref[...], mod_ref[SH1:SH1 + 1], mod_ref[SC1:SC1 + 1]).astype(BF16)
        yr.append(_dot(kc_ref[...], h).reshape(n1, FFT_B2, d))
        yi.append(_dot(ks_ref[...], h).reshape(n1, FFT_B2, d))
    yr_ref[...] = jnp.concatenate(yr, axis=1).astype(BF16)
    yi_ref[...] = jnp.concatenate(yi, axis=1).astype(BF16)


def _fft_stage2_kernel(yr_ref, yi_ref, m_ref, cc_ref, sc_ref, p_ref, w_ref, x_ref, mod_ref, o_ref,
                       f_ref, fp_ref):
    b2 = m_ref.shape[0]
    tn = o_ref.shape[2]

    @pl.when(pl.program_id(1) == 0)
    def _():
        for q in range(b2):
            rows = slice(q * FFT_N2, (q + 1) * FFT_N2)
            ys = jnp.concatenate([yr_ref[rows], yi_ref[rows]], axis=0)
            xs = _dot(m_ref[q], ys).astype(BF16)
            f_ref[rows] = _channel_dft(xs[:FFT_N2], xs[FFT_N2:], cc_ref, sc_ref, _dot).astype(BF16)
        pr = p_ref.shape[0]
        kb = pr // b2
        for b in range(FFT_N2 // kb):
            g = jnp.concatenate([f_ref[q * FFT_N2 + b * kb:q * FFT_N2 + (b + 1) * kb, :]
                                 for q in range(b2)], axis=0)
            fp_ref[b * pr:(b + 1) * pr, :] = _dot(p_ref[...], g).astype(BF16)

    y = _dot(fp_ref[...], w_ref[...])
    o_ref[...] = x_ref[...] + mod_ref[G1:G1 + 1].reshape(1, 1, tn) * y.reshape(FFT_N2, b2, tn)


def _dft_small_kernel(x_ref, mod_ref, g1_ref, m_ref, cc_ref, sc_ref, w_ref, o_ref):
    n = x_ref.shape[0]
    x = x_ref[...]
    h = _norm_mod(x, g1_ref[...], mod_ref[SH1:SH1 + 1], mod_ref[SC1:SC1 + 1])
    xs = _dot_hi(m_ref[...], h)
    fmix = _channel_dft(xs[:n], xs[n:], cc_ref, sc_ref, _dot_hi).astype(BF16)
    o_ref[...] = x + mod_ref[G1:G1 + 1] * _dot(fmix, w_ref[...])


def _cos_sin(num, den, scale):
    ang = (2.0 * np.pi / den) * (num % den).astype(np.float64)
    return (np.cos(ang) * scale).astype(np.float32), (np.sin(ang) * scale).astype(np.float32)


def _dft_tables(n):
    gd = FNO_GROUP_DIM
    a = np.arange(gd, dtype=np.int64)
    cc, sc = _cos_sin(a[:, None] * a[None, :], gd, gd ** -0.5)
    if n % (FFT_N2 * FFT_B2) == 0 and n > 2 * FFT_N2:
        n1 = n // FFT_N2
        j1 = np.arange(n1, dtype=np.int64)
        c1, s1 = _cos_sin(j1[:, None] * j1[None, :], n1, n1 ** -0.5)
        eye = np.eye(FFT_B2, dtype=np.float32)
        k = j1[:, None, None] + n1 * np.arange(FFT_N2, dtype=np.int64)[None, :, None]
        j2 = np.arange(FFT_N2, dtype=np.int64)[None, None, :]
        mc, ms = _cos_sin(k * j2, n, FFT_N2 ** -0.5)
        m2 = np.concatenate([np.concatenate([mc, ms], axis=2),
                             np.concatenate([-ms, mc], axis=2)], axis=1)
        rows = np.arange(BF16_ROWS * FFT_B2)
        perm = np.zeros((BF16_ROWS * FFT_B2,) * 2, np.float32)
        perm[rows, (rows % FFT_B2) * BF16_ROWS + rows // FFT_B2] = 1.0
        tabs = {"kc": np.kron(c1, eye), "ks": np.kron(-s1, eye), "m2": m2, "perm": perm,
                "cc": cc, "sc": sc}
        return {k_: jnp.asarray(v).astype(BF16) for k_, v in tabs.items()}
    j = np.arange(n, dtype=np.int64)
    c, s = _cos_sin(j[:, None] * j[None, :], n, n ** -0.5)
    tabs = {"m": np.concatenate([c, -s], axis=0), "cc": cc, "sc": sc}
    return {k_: jnp.asarray(v) for k_, v in tabs.items()}


def _fourier_long(x, mod, g1, layer, w, tabs, tn):
    n, d = x.shape
    n1 = n // FFT_N2
    nb = n1 * FFT_B2
    yr, yi = pl.pallas_call(
        _fft_stage1_kernel,
        grid=(FFT_N2 // FFT_B1,),
        in_specs=[
            pl.BlockSpec((n1, FFT_B1, d), lambda s: (0, s, 0)),
            _const_spec(mod.shape), _const_spec(g1.shape),
            _const_spec((nb, nb)), _const_spec((nb, nb)),
        ],
        out_specs=[pl.BlockSpec((n1, FFT_B1, d), lambda s: (0, s, 0))] * 2,
        out_shape=[jax.ShapeDtypeStruct((n1, FFT_N2, d), BF16)] * 2,
        compiler_params=_params("arbitrary"),
        name="fft_stage1",
    )(x.reshape(n1, FFT_N2, d), mod, g1, tabs["kc"], tabs["ks"])
    rows = FFT_N2 * FFT_B2
    out = pl.pallas_call(
        _fft_stage2_kernel,
        grid=(n1 // FFT_B2, d // tn),
        in_specs=[
            pl.BlockSpec((rows, d), lambda a, j: (a, 0)),
            pl.BlockSpec((rows, d), lambda a, j: (a, 0)),
            pl.BlockSpec((FFT_B2, 2 * FFT_N2, 2 * FFT_N2), lambda a, j: (a, 0, 0)),
            _const_spec(tabs["cc"].shape), _const_spec(tabs["sc"].shape),
            _const_spec(tabs["perm"].shape),
            pl.BlockSpec((None, d, tn), lambda a, j: (layer, 0, j)),
            pl.BlockSpec((FFT_N2, FFT_B2, tn), lambda a, j: (0, a, j)),
            pl.BlockSpec((N_MOD, tn), lambda a, j: (0, j)),
        ],
        out_specs=pl.BlockSpec((FFT_N2, FFT_B2, tn), lambda a, j: (0, a, j)),
        out_shape=jax.ShapeDtypeStruct((FFT_N2, n1, d), F32),
        scratch_shapes=[pltpu.VMEM((rows, d), BF16), pltpu.VMEM((rows, d), BF16)],
        compiler_params=_params("arbitrary", "arbitrary"),
        name="fft_stage2",
    )(yr.reshape(n, d), yi.reshape(n, d), tabs["m2"], tabs["cc"], tabs["sc"], tabs["perm"], w,
      x.reshape(FFT_N2, n1, d), mod)
    return out.reshape(n, d)


def _fourier_short(x, mod, g1, layer, w, tabs):
    n, d = x.shape
    return pl.pallas_call(
        _dft_small_kernel,
        grid=(1,),
        in_specs=[_const_spec(a.shape) for a in (x, mod, g1, tabs["m"], tabs["cc"], tabs["sc"])]
        + [pl.BlockSpec((None, d, d), lambda i: (layer, 0, 0))],
        out_specs=pl.BlockSpec((n, d), lambda i: (0, 0)),
        out_shape=jax.ShapeDtypeStruct((n, d), F32),
        compiler_params=_params("arbitrary"),
        name="dft_ctx",
    )(x, mod, g1, tabs["m"], tabs["cc"], tabs["sc"], w)


def _fourier(x, mod, g1, layer, w):
    tabs = _dft_tables(x.shape[0])
    if "m2" in tabs:
        return _fourier_long(x, mod, g1, layer, w, tabs, 512)
    return _fourier_short(x, mod, g1, layer, w, tabs)


def _rot_cols(w):
    q = QK_ROPE // 4
    a1, a2, b1, b2 = w[..., :q], w[..., q:2 * q], w[..., 2 * q:3 * q], w[..., 3 * q:]
    return jnp.concatenate([-a2, a1, -b2, b1], axis=-1)


def _pad_lanes(w):
    return jnp.pad(w, [(0, 0)] * (w.ndim - 1) + [(0, LANES - w.shape[-1])])


def _mla_weights(w_dqkv, g_q, g_kv, w_uq, w_ukv):
    hd = N_HEADS * LANES
    kr = w_dqkv[:, Q_LORA + KV_LORA:]
    wd = jnp.concatenate([w_dqkv[:, :Q_LORA + KV_LORA], _pad_lanes(kr), _pad_lanes(_rot_cols(kr))],
                         axis=1).astype(BF16)
    q3 = w_uq.reshape(Q_LORA, N_HEADS, QK_NOPE + QK_ROPE)
    qr = q3[:, :, QK_NOPE:]
    wq = jnp.concatenate([q3[:, :, :QK_NOPE].reshape(Q_LORA, hd),
                          qr.reshape(Q_LORA, N_HEADS * QK_ROPE),
                          _rot_cols(qr).reshape(Q_LORA, N_HEADS * QK_ROPE)], axis=1).astype(BF16)
    kv3 = w_ukv.reshape(KV_LORA, N_HEADS, QK_NOPE + V_DIM)
    wk = kv3[:, :, :QK_NOPE].reshape(KV_LORA, hd).astype(BF16)
    wvT = kv3[:, :, QK_NOPE:].reshape(KV_LORA, N_HEADS * V_DIM).T.astype(BF16)
    return wd, g_q.reshape(1, -1), g_kv.reshape(1, -1), wq, wk, wvT


def _rope_tables(n):
    rows = n // GRID_W
    r, col = jnp.meshgrid(jnp.arange(rows, dtype=F32), jnp.arange(GRID_W, dtype=F32), indexing="ij")
    half = QK_ROPE // 2
    inv_freq = jnp.power(ROPE_BASE, -jnp.arange(0, half, 2, dtype=F32) / half)
    ang_r = r.reshape(-1)[:, None] * inv_freq
    ang_c = col.reshape(-1)[:, None] * inv_freq
    ang = jnp.concatenate([ang_r, ang_r, ang_c, ang_c], axis=-1)
    return jnp.tile(jnp.cos(ang), (1, 2)), jnp.tile(jnp.sin(ang), (1, 2))


def _row_tile(n, pref):
    return pref if n % pref == 0 else n


def kernel(x, c, ctx, c_ctx, ada_w, ada_b, norm1_g, norm2_g, mla_w_dqkv, mla_q_norm_g, mla_kv_norm_g,
           mla_w_uq, mla_w_ukv, mla_w_o, fno_w, ffn_w_up, ffn_conv_w, ffn_conv_b, ffn_w_down,
           final_norm_g):
    assert x.shape[0] == 1 and c.shape[0] == 1 and ctx.shape[0] == 1
    xs = x[0]
    cs = ctx[0]
    n, nc = xs.shape[0], cs.shape[0]
    mods = _ada_all(jnp.concatenate([c, c_ctx[None, :]], axis=0), ada_w, ada_b)
    cos, sin = _rope_tables(n)
    ones_c = jnp.ones((nc, LANES), F32)
    zeros_c = jnp.zeros((nc, LANES), F32)
    fin_g = final_norm_g.reshape(1, -1)
    w_fno = fno_w.astype(BF16)
    conv_b = ffn_conv_b.reshape(DEPTH, 1, -1)
    d_ff2 = ffn_w_up.shape[2]
    up2d = ffn_w_up.reshape(DEPTH * D_MODEL, d_ff2)
    down2d = ffn_w_down.reshape(DEPTH * D_FF, D_MODEL)
    assert DEPTH % N_MIXERS == 0 and MIXER_MLA == 0

    for i in range(DEPTH):
        kind = i % N_MIXERS
        j = i // N_MIXERS
        ctx_later = any(l % N_MIXERS == MIXER_MLA for l in range(i + 1, DEPTH))
        mod_x, mod_c = mods[i, 0], mods[i, 1]
        g1 = norm1_g[i].reshape(1, -1)
        g2 = norm2_g[i].reshape(1, -1)
        if kind == MIXER_MLA:
            wts = _mla_weights(mla_w_dqkv[j], mla_q_norm_g[j], mla_kv_norm_g[j], mla_w_uq[j], mla_w_ukv[j])
            w_o = mla_w_o[j].astype(BF16)
            qx, kx, vTx = _mla_proj(xs, mod_x, g1, wts, cos, sin, _row_tile(n, 256))
            qc, kc, vTc = _mla_proj(cs, mod_c, g1, wts, ones_c, zeros_c, _row_tile(nc, 256))
            ox, (wu, wdn) = _flash(qx, [(kc, vTc), (kx, vTx)], _row_tile(n, 2048), 512,
                                   casts=[(up2d, i * D_MODEL, N_MIXERS * D_MODEL),
                                          (down2d, i * D_FF, N_MIXERS * D_FF)])
            w_up = wu.reshape(N_MIXERS, D_MODEL, d_ff2)
            w_down = wdn.reshape(N_MIXERS, D_FF, D_MODEL)
            xs = _outproj(ox, w_o, xs, mod_x, _row_tile(n, 512))
            if ctx_later:
                oc, _ = _flash(qc, [(kc, vTc)], _row_tile(nc, 256), 512)
                cs = _outproj(oc, w_o, cs, mod_c, _row_tile(nc, 256))
        else:
            xs = _fourier(xs, mod_x, g1, j, w_fno)
            if ctx_later:
                cs = _fourier(cs, mod_c, g1, j, w_fno)
        xs = _ffn(xs, mod_x, g2, i, kind, w_up, ffn_conv_w, conv_b, w_down, fin_g, _row_tile(n, 512), 512,
                  final=(i == DEPTH - 1))
        if ctx_later:
            cs = _ffn(cs, mod_c, g2, i, kind, w_up, ffn_conv_w, conv_b, w_down, fin_g, _row_tile(nc, 256),
                      512, final=False)
    return xs[None]
```
